```python
import math
import jax, jax.numpy as jnp
from jax import lax
import numpy as np

D_MODEL = 1024
BATCH = 4
SEQ = 4096
DEPTH = 2

BRANCH_WIDTH = 512
N_BRANCH = 3
S5_GROUP = 16
S5_GROUPS = BRANCH_WIDTH // S5_GROUP
S5_STATE = 64
S5_DT_MIN = 1e-3
S5_DT_MAX = 1e-1
S5_EIG_MAX = -1e-4
HG_HEADS = 4
HG_DK = 128
HG_DV = BRANCH_WIDTH // HG_HEADS
HG_KEY_WIDTH = HG_HEADS * HG_DK
HG_CHUNK = 64
RG_BLOCKS = 8
RG_BLOCK = BRANCH_WIDTH // RG_BLOCKS
RG_C = 8.0
CONV_WIDTH = 4
D_FF = 2816
EPS = 1e-6
IN_SPLIT_SIZES = (BRANCH_WIDTH, HG_KEY_WIDTH, HG_KEY_WIDTH, BRANCH_WIDTH, BRANCH_WIDTH, BRANCH_WIDTH, BRANCH_WIDTH)
IN_TOTAL = BRANCH_WIDTH + 2 * HG_KEY_WIDTH + 2 * BRANCH_WIDTH + 2 * BRANCH_WIDTH + N_BRANCH * D_MODEL

kernel_name = 'hybrid_s5_hgrn2_rglru_macaron'


def rms_norm(x, w):
    xf = x.astype(jnp.float32)
    y = xf * lax.rsqrt(jnp.mean(xf * xf, axis=-1, keepdims=True) + EPS)
    return (y * w.astype(jnp.float32)).astype(x.dtype)


def swiglu(h, w_gate, w_up, w_down):
    return (jax.nn.silu(h @ w_gate) * (h @ w_up)) @ w_down


def _complex_affine_combine(e1, e2):
    a1r, a1i, b1r, b1i = e1
    a2r, a2i, b2r, b2i = e2
    return (a2r * a1r - a2i * a1i,
            a2r * a1i + a2i * a1r,
            a2r * b1r - a2i * b1i + b2r,
            a2r * b1i + a2i * b1r + b2i)


def _real_affine_combine(e1, e2):
    a1, b1 = e1
    a2, b2 = e2
    return (a2 * a1, a2 * b1 + b2)


def s5_mixer(u, lam_re, lam_im, log_dt, b_re, b_im, c_re, c_im, d_skip, glu_w, glu_b):
    f32 = jnp.float32
    bsz, seq, _ = u.shape
    uf = u.astype(f32)
    ug = uf.reshape(bsz, seq, S5_GROUPS, S5_GROUP)
    lr = jnp.minimum(lam_re.astype(f32), S5_EIG_MAX)
    li = lam_im.astype(f32)
    dt = jnp.exp(log_dt.astype(f32))[:, None]
    mag = jnp.exp(lr * dt)
    ar = mag * jnp.cos(li * dt)
    ai = mag * jnp.sin(li * dt)
    den = lr * lr + li * li
    fr = ((ar - 1.0) * lr + ai * li) / den
    fi = (ai * lr - (ar - 1.0) * li) / den
    br, bi = b_re.astype(f32), b_im.astype(f32)
    bbr = fr[..., None] * br - fi[..., None] * bi
    bbi = fr[..., None] * bi + fi[..., None] * br
    bu_r = jnp.einsum('blgc,gpc->blgp', ug, bbr)
    bu_i = jnp.einsum('blgc,gpc->blgp', ug, bbi)
    a_r = jnp.broadcast_to(ar, bu_r.shape)
    a_i = jnp.broadcast_to(ai, bu_i.shape)
    _, _, xr, xi = lax.associative_scan(_complex_affine_combine, (a_r, a_i, bu_r, bu_i), axis=1)
    y = (jnp.einsum('blgp,gcp->blgc', xr, c_re.astype(f32))
         - jnp.einsum('blgp,gcp->blgc', xi, c_im.astype(f32)))
    y = y.reshape(bsz, seq, BRANCH_WIDTH) + d_skip.astype(f32) * uf
    z = jax.nn.gelu(y)
    out = z * jax.nn.sigmoid(z @ glu_w.astype(f32) + glu_b.astype(f32))
    return out.astype(u.dtype)


def hgrn2_mixer(q, z_f, v, g, lb, norm_w):
    f32 = jnp.float32
    bsz, seq, _ = q.shape
    n_chunks = seq // HG_CHUNK
    lb = lb.astype(f32).reshape(HG_HEADS, HG_DK)
    qh = jax.nn.silu(q.astype(f32)).reshape(bsz, seq, HG_HEADS, HG_DK)
    zf = z_f.astype(f32).reshape(bsz, seq, HG_HEADS, HG_DK)
    log_f = jnp.log(lb + (1.0 - lb) * jax.nn.sigmoid(zf))
    kh = (1.0 - lb) * jax.nn.sigmoid(-zf)
    vh = v.astype(f32).reshape(bsz, seq, HG_HEADS, HG_DV)

    def to_chunks(t):
        return t.reshape(bsz, n_chunks, HG_CHUNK, HG_HEADS, t.shape[-1]).transpose(1, 0, 3, 2, 4)

    causal = jnp.tril(jnp.ones((HG_CHUNK, HG_CHUNK), dtype=bool))[:, :, None]

    def chunk_step(state, inp):
        qc, kc, vc, lfc = inp
        b = jnp.cumsum(lfc, axis=2)
        o_inter = jnp.einsum('bhcd,bhde->bhce', qc * jnp.exp(b), state)
        diff = b[:, :, :, None, :] - b[:, :, None, :, :]
        decay = jnp.where(causal, jnp.exp(jnp.where(causal, diff, 0.0)), 0.0)
        scores = jnp.einsum('bhtd,bhtsd,bhsd->bhts', qc, decay, kc)
        o_intra = jnp.einsum('bhts,bhse->bhte', scores, vc)
        b_last = b[:, :, -1:, :]
        new_state = (jnp.exp(b_last[:, :, 0, :])[..., None] * state
                     + jnp.einsum('bhsd,bhse->bhde', kc * jnp.exp(b_last - b), vc))
        return new_state, o_inter + o_intra

    s0 = jnp.zeros((bsz, HG_HEADS, HG_DK, HG_DV), f32)
    _, o = lax.scan(chunk_step, s0, (to_chunks(qh), to_chunks(kh), to_chunks(vh), to_chunks(log_f)))
    o = o.transpose(1, 0, 3, 2, 4).reshape(bsz, seq, HG_HEADS, HG_DV)
    o = o * lax.rsqrt(jnp.mean(o * o, axis=-1, keepdims=True) + EPS)
    o = o * norm_w.astype(f32).reshape(HG_HEADS, HG_DV)
    out = o.reshape(bsz, seq, BRANCH_WIDTH) * jax.nn.silu(g.astype(f32))
    return out.astype(q.dtype)


def rglru_mixer(xb, gate, conv_w, conv_b, wa, ba, wx, bx, lam):
    f32 = jnp.float32
    bsz, seq, _ = xb.shape
    xc = lax.conv_general_dilated(
        xb, conv_w[:, None, :], window_strides=(1,), padding=[(CONV_WIDTH - 1, 0)],
        dimension_numbers=('NWC', 'WIO', 'NWC'), feature_group_count=BRANCH_WIDTH) + conv_b
    xcf = xc.astype(f32)
    xblk = xcf.reshape(bsz, seq, RG_BLOCKS, RG_BLOCK)
    r = jax.nn.sigmoid(jnp.einsum('blhi,hij->blhj', xblk, wa.astype(f32)).reshape(bsz, seq, BRANCH_WIDTH) + ba.astype(f32))
    i = jax.nn.sigmoid(jnp.einsum('blhi,hij->blhj', xblk, wx.astype(f32)).reshape(bsz, seq, BRANCH_WIDTH) + bx.astype(f32))
    log_a = -RG_C * jax.nn.softplus(-lam.astype(f32)) * r
    a = jnp.exp(log_a)
    b = jnp.sqrt(-jnp.expm1(2.0 * log_a)) * (i * xcf)
    _, hseq = lax.associative_scan(_real_affine_combine, (a, b), axis=1)
    return (hseq * jax.nn.gelu(gate.astype(f32))).astype(xb.dtype)


def hybrid_mixer(h, w_in, branch_proj, w_out,
                 s5_lambda_re, s5_lambda_im, s5_log_dt, s5_b_re, s5_b_im, s5_c_re, s5_c_im,
                 s5_d, s5_glu_w, s5_glu_b, hg_lb, hg_norm_w,
                 rg_conv_w, rg_conv_b, rg_wa, rg_ba, rg_wx, rg_bx, rg_lambda):
    bsz, seq, _ = h.shape
    proj = h @ w_in
    points, acc = [], 0
    for s in IN_SPLIT_SIZES:
        acc += s
        points.append(acc)
    u_a, q_b, f_b, v_b, g_b, x_c, gate_c, gate_merge = jnp.split(proj, points, axis=-1)
    y_a = s5_mixer(u_a, s5_lambda_re, s5_lambda_im, s5_log_dt, s5_b_re, s5_b_im,
                   s5_c_re, s5_c_im, s5_d, s5_glu_w, s5_glu_b)
    y_b = hgrn2_mixer(q_b, f_b, v_b, g_b, hg_lb, hg_norm_w)
    y_c = rglru_mixer(x_c, gate_c, rg_conv_w, rg_conv_b, rg_wa, rg_ba, rg_wx, rg_bx, rg_lambda)
    branches = jnp.stack([y_a, y_b, y_c], axis=2)
    up = jnp.einsum('blnw,nwd->blnd', branches, branch_proj)
    gates = jax.nn.sigmoid(gate_merge.astype(jnp.float32)).reshape(bsz, seq, N_BRANCH, D_MODEL)
    merged = jnp.sum(gates * up.astype(jnp.float32), axis=2).astype(h.dtype)
    return merged @ w_out


def setup_inputs(seed: int = 0) -> dict:
    key = jax.random.key(seed)
    ks = jax.random.split(key, 32)
    f32 = jnp.float32

    def nrm(k, shape, scale):
        return jax.random.normal(k, shape, f32) * scale

    x = nrm(ks[0], (BATCH, SEQ, D_MODEL), 1.0)
    norm_w = 1.0 + nrm(ks[1], (DEPTH, 3, D_MODEL), 0.02)
    final_norm_w = 1.0 + nrm(ks[2], (D_MODEL,), 0.02)
    ffn_gate = nrm(ks[3], (DEPTH, 2, D_MODEL, D_FF), D_MODEL ** -0.5)
    ffn_up = nrm(ks[4], (DEPTH, 2, D_MODEL, D_FF), D_MODEL ** -0.5)
    ffn_down = nrm(ks[5], (DEPTH, 2, D_FF, D_MODEL), D_FF ** -0.5)
    w_in = nrm(ks[6], (DEPTH, D_MODEL, IN_TOTAL), D_MODEL ** -0.5)
    branch_proj = nrm(ks[7], (DEPTH, N_BRANCH, BRANCH_WIDTH, D_MODEL), BRANCH_WIDTH ** -0.5)
    w_out = nrm(ks[8], (DEPTH, D_MODEL, D_MODEL), D_MODEL ** -0.5)
    s5_lambda_re = -0.5 + nrm(ks[9], (DEPTH, S5_GROUPS, S5_STATE), 0.01)
    s5_lambda_im = (math.pi * jnp.arange(S5_STATE, dtype=f32)) + nrm(ks[10], (DEPTH, S5_GROUPS, S5_STATE), 0.01)
    s5_log_dt = jax.random.uniform(ks[11], (DEPTH, S5_GROUPS), f32, math.log(S5_DT_MIN), math.log(S5_DT_MAX))
    s5_b_re = nrm(ks[12], (DEPTH, S5_GROUPS, S5_STATE, S5_GROUP), (2.0 * S5_GROUP) ** -0.5)
    s5_b_im = nrm(ks[13], (DEPTH, S5_GROUPS, S5_STATE, S5_GROUP), (2.0 * S5_GROUP) ** -0.5)
    s5_c_re = nrm(ks[14], (DEPTH, S5_GROUPS, S5_GROUP, S5_STATE), S5_STATE ** -0.5)
    s5_c_im = nrm(ks[15], (DEPTH, S5_GROUPS, S5_GROUP, S5_STATE), S5_STATE ** -0.5)
    s5_d = nrm(ks[16], (DEPTH, BRANCH_WIDTH), 1.0)
    s5_glu_w = nrm(ks[17], (DEPTH, BRANCH_WIDTH, BRANCH_WIDTH), BRANCH_WIDTH ** -0.5)
    s5_glu_b = nrm(ks[18], (DEPTH, BRANCH_WIDTH), 0.01)
    hg_lb_logits = 1.0 + nrm(ks[19], (DEPTH, HG_KEY_WIDTH), 0.1)
    hg_norm_w = 1.0 + nrm(ks[20], (DEPTH, BRANCH_WIDTH), 0.02)
    rg_conv_w = nrm(ks[21], (DEPTH, CONV_WIDTH, BRANCH_WIDTH), CONV_WIDTH ** -0.5)
    rg_conv_b = nrm(ks[22], (DEPTH, BRANCH_WIDTH), 0.01)
    rg_wa = nrm(ks[23], (DEPTH, RG_BLOCKS, RG_BLOCK, RG_BLOCK), RG_BLOCK ** -0.5)
    rg_ba = nrm(ks[24], (DEPTH, BRANCH_WIDTH), 0.01)
    rg_wx = nrm(ks[25], (DEPTH, RG_BLOCKS, RG_BLOCK, RG_BLOCK), RG_BLOCK ** -0.5)
    rg_bx = nrm(ks[26], (DEPTH, BRANCH_WIDTH), 0.01)
    a_c = jax.random.uniform(ks[27], (DEPTH, BRANCH_WIDTH), f32, 0.9, 0.999)
    s = a_c ** (1.0 / RG_C)
    rg_lambda = jnp.log(s) - jnp.log1p(-s)
    return {'x': x, 'norm_w': norm_w, 'final_norm_w': final_norm_w,
            'ffn_gate': ffn_gate, 'ffn_up': ffn_up, 'ffn_down': ffn_down,
            'w_in': w_in, 'branch_proj': branch_proj, 'w_out': w_out,
            's5_lambda_re': s5_lambda_re, 's5_lambda_im': s5_lambda_im, 's5_log_dt': s5_log_dt,
            's5_b_re': s5_b_re, 's5_b_im': s5_b_im, 's5_c_re': s5_c_re, 's5_c_im': s5_c_im,
            's5_d': s5_d, 's5_glu_w': s5_glu_w, 's5_glu_b': s5_glu_b,
            'hg_lb_logits': hg_lb_logits, 'hg_norm_w': hg_norm_w,
            'rg_conv_w': rg_conv_w, 'rg_conv_b': rg_conv_b, 'rg_wa': rg_wa, 'rg_ba': rg_ba,
            'rg_wx': rg_wx, 'rg_bx': rg_bx, 'rg_lambda': rg_lambda}


def reference(x, norm_w, final_norm_w, ffn_gate, ffn_up, ffn_down, w_in, branch_proj, w_out,
              s5_lambda_re, s5_lambda_im, s5_log_dt, s5_b_re, s5_b_im, s5_c_re, s5_c_im,
              s5_d, s5_glu_w, s5_glu_b, hg_lb_logits, hg_norm_w,
              rg_conv_w, rg_conv_b, rg_wa, rg_ba, rg_wx, rg_bx, rg_lambda):
    p = jax.nn.softmax(hg_lb_logits.astype(jnp.float32), axis=0)
    lower_bounds = jnp.cumsum(p, axis=0) - p[0]
    for l in range(DEPTH):
        h = rms_norm(x, norm_w[l, 0])
        x = x + 0.5 * swiglu(h, ffn_gate[l, 0], ffn_up[l, 0], ffn_down[l, 0])
        h = rms_norm(x, norm_w[l, 1])
        x = x + hybrid_mixer(h, w_in[l], branch_proj[l], w_out[l],
                             s5_lambda_re[l], s5_lambda_im[l], s5_log_dt[l], s5_b_re[l], s5_b_im[l],
                             s5_c_re[l], s5_c_im[l], s5_d[l], s5_glu_w[l], s5_glu_b[l],
                             lower_bounds[l], hg_norm_w[l],
                             rg_conv_w[l], rg_conv_b[l], rg_wa[l], rg_ba[l], rg_wx[l], rg_bx[l], rg_lambda[l])
        h = rms_norm(x, norm_w[l, 2])
        x = x + 0.5 * swiglu(h, ffn_gate[l, 1], ffn_up[l, 1], ffn_down[l, 1])
    return rms_norm(x, final_norm_w)
```

```python
import functools
import math

import jax
import jax.numpy as jnp
from jax import lax
from jax.experimental import pallas as pl
from jax.experimental.pallas import tpu as pltpu

F32 = jnp.float32
BF16 = jnp.bfloat16

LANES = 128
SUBLANES = 8
VMEM_LIMIT_BYTES = 56 * 1024 * 1024

EPS = 1e-6
BRANCH_WIDTH = 512
N_BRANCH = 3
S5_GROUP = 16
S5_STATE = 64
S5_EIG_MAX = -1e-4
S5_GROUPS_PER_TILE = LANES // S5_GROUP
S5_TILE_STATES = S5_GROUPS_PER_TILE * S5_STATE
HG_HEADS = 4
HG_DK = 128
HG_DV = 128
HG_CHUNK = 64
RG_BLOCKS = 8
RG_BLOCK = 64
RG_C = 8.0
CONV_WIDTH = 4

FFN_ROWS = 512
FFN_FCHUNK = 256
PROJ_ROWS = 1024
PROJ_COLS = 1664
SEQ_ROWS = 256
MERGE_ROWS = 512


def _rmsnorm(x, w):
    ms = jnp.mean(x * x, axis=-1, keepdims=True)
    return x * lax.rsqrt(ms + EPS) * w


def _const_spec(shape, single_buffer=False):
    nd = len(shape)
    kw = {"pipeline_mode": pl.Buffered(1)} if single_buffer else {}
    return pl.BlockSpec(shape, lambda *_: (0,) * nd, **kw)


def _params(*sem):
    return pltpu.CompilerParams(dimension_semantics=sem, vmem_limit_bytes=VMEM_LIMIT_BYTES)


def _ffn_body(x_ref, nw_ref, wg_ref, wu_ref, wd_ref, *rest, final):
    o_ref = rest[-1]
    x = x_ref[...]
    h = _rmsnorm(x, nw_ref[...]).astype(BF16)
    acc = jnp.zeros(x.shape, F32)
    for c in range(wg_ref.shape[0]):
        g = jnp.dot(h, wg_ref[c], preferred_element_type=F32)
        u = jnp.dot(h, wu_ref[c], preferred_element_type=F32)
        a = (g * jax.nn.sigmoid(g) * u).astype(BF16)
        acc = acc + jnp.dot(a, wd_ref[c], preferred_element_type=F32)
    y = x + 0.5 * acc
    if final:
        y = _rmsnorm(y, rest[0][...])
    o_ref[...] = y


def _ffn(x, nw, wg, wu, wd, final_w=None):
    n, d = x.shape
    nf, _, fc = wg.shape
    final = final_w is not None
    in_specs = [
        pl.BlockSpec((FFN_ROWS, d), lambda i: (i, 0)),
        _const_spec((1, d)),
        _const_spec((nf, d, fc), single_buffer=True),
        _const_spec((nf, d, fc), single_buffer=True),
        _const_spec((nf, fc, d), single_buffer=True),
    ]
    args = [x, nw, wg, wu, wd]
    if final:
        in_specs.append(_const_spec((1, d)))
        args.append(final_w)
    return pl.pallas_call(
        functools.partial(_ffn_body, final=final),
        grid=(n // FFN_ROWS,),
        in_specs=in_specs,
        out_specs=pl.BlockSpec((FFN_ROWS, d), lambda i: (i, 0)),
        out_shape=jax.ShapeDtypeStruct((n, d), F32),
        compiler_params=_params("parallel"),
        name="ffn_final" if final else "ffn",
    )(*args)


def _proj_body(x_ref, nw_ref, w_ref, o_ref, h_ref):
    @pl.when(pl.program_id(1) == 0)
    def _():
        h_ref[...] = _rmsnorm(x_ref[...], nw_ref[...]).astype(BF16)

    o_ref[...] = jnp.dot(h_ref[...], w_ref[...], preferred_element_type=F32)


def _proj(x, nw, w):
    n, d = x.shape
    cols = w.shape[1]
    return pl.pallas_call(
        _proj_body,
        grid=(n // PROJ_ROWS, cols // PROJ_COLS),
        in_specs=[
            pl.BlockSpec((PROJ_ROWS, d), lambda i, j: (i, 0)),
            _const_spec((1, d)),
            pl.BlockSpec((d, PROJ_COLS), lambda i, j: (0, j)),
        ],
        out_specs=pl.BlockSpec((PROJ_ROWS, PROJ_COLS), lambda i, j: (i, j)),
        out_shape=jax.ShapeDtypeStruct((n, cols), F32),
        scratch_shapes=[pltpu.VMEM((PROJ_ROWS, d), BF16)],
        compiler_params=_params("parallel", "arbitrary"),
        name="mixer_in_proj",
    )(x, nw, w)


def _bcast_last_row(v):
    return jnp.broadcast_to(v[SUBLANES - 1:SUBLANES, :], v.shape)


def _s5_body(u_ref, bmat_ref, cmat_ref, tre_ref, tim_ref, d_ref, gw_ref, gb_ref, o_ref,
             cre_ref, cim_ref, y_ref):
    @pl.when(pl.program_id(1) == 0)
    def _():
        cre_ref[...] = jnp.zeros(cre_ref.shape, F32)
        cim_ref[...] = jnp.zeros(cim_ref.shape, F32)

    rows = u_ref.shape[0]
    n_tiles = bmat_ref.shape[0]
    ts = S5_TILE_STATES
    u = u_ref[...]
    for j in range(n_tiles):
        uj = u[:, j * LANES:(j + 1) * LANES].astype(BF16)
        bu = jnp.dot(uj, bmat_ref[j], preferred_element_type=F32)
        lanes = slice(j * ts, (j + 1) * ts)
        cr = cre_ref[:, lanes]
        ci = cim_ref[:, lanes]
        pr = tre_ref[3, :, lanes]
        pi = tim_ref[3, :, lanes]
        xr_blocks, xi_blocks = [], []
        for k in range(rows // SUBLANES):
            rs = slice(k * SUBLANES, (k + 1) * SUBLANES)
            xr = bu[rs, :ts]
            xi = bu[rs, ts:]
            for s in range(3):
                d = 1 << s
                ar = tre_ref[s, :, lanes]
                ai = tim_ref[s, :, lanes]
                sr = pltpu.roll(xr, d, axis=0)
                si = pltpu.roll(xi, d, axis=0)
                xr, xi = xr + ar * sr - ai * si, xi + ar * si + ai * sr
            xr, xi = xr + pr * cr - pi * ci, xi + pr * ci + pi * cr
            cr = _bcast_last_row(xr)
            ci = _bcast_last_row(xi)
            xr_blocks.append(xr)
            xi_blocks.append(xi)
        cre_ref[:, lanes] = cr
        cim_ref[:, lanes] = ci
        xs = jnp.concatenate(
            [jnp.concatenate(xr_blocks, axis=0), jnp.concatenate(xi_blocks, axis=0)], axis=1)
        y_ref[:, j * LANES:(j + 1) * LANES] = jnp.dot(
            xs.astype(BF16), cmat_ref[j], preferred_element_type=F32)
    y = y_ref[...] + d_ref[...] * u
    z = jax.nn.gelu(y)
    gate = jnp.dot(z.astype(BF16), gw_ref[...], preferred_element_type=F32) + gb_ref[...]
    o_ref[...] = z * jax.nn.sigmoid(gate)


def _s5(proj, col_block, batch, seq, tabs, d_skip, glu_w, glu_b):
    n = proj.shape[0]
    bw = BRANCH_WIDTH
    steps = seq // SEQ_ROWS
    n_states = tabs["tre"].shape[-1]
    return pl.pallas_call(
        _s5_body,
        grid=(batch, steps),
        in_specs=[
            pl.BlockSpec((SEQ_ROWS, bw), lambda b, l: (b * steps + l, col_block)),
            _const_spec(tabs["bmat"].shape),
            _const_spec(tabs["cmat"].shape),
            _const_spec(tabs["tre"].shape),
            _const_spec(tabs["tim"].shape),
            _const_spec((1, bw)),
            _const_spec((bw, bw)),
            _const_spec((1, bw)),
        ],
        out_specs=pl.BlockSpec((SEQ_ROWS, bw), lambda b, l: (b * steps + l, 0)),
        out_shape=jax.ShapeDtypeStruct((n, bw), F32),
        scratch_shapes=[
            pltpu.VMEM((SUBLANES, n_states), F32),
            pltpu.VMEM((SUBLANES, n_states), F32),
            pltpu.VMEM((SEQ_ROWS, bw), F32),
        ],
        compiler_params=_params("parallel", "arbitrary"),
        name="s5_mixer",
    )(proj, tabs["bmat"], tabs["cmat"], tabs["tre"], tabs["tim"], d_skip, glu_w, glu_b)


def _rglru_body(x_ref, g_ref, cw_ref, cb_ref, wa_ref, ba_ref, wx_ref, bx_ref, c1_ref, o_ref,
                xbuf_ref, carry_ref):
    rows = x_ref.shape[0]
    first = pl.program_id(1) == 0

    @pl.when(first)
    def _():
        xbuf_ref[0:SUBLANES, :] = jnp.zeros((SUBLANES, x_ref.shape[1]), F32)
        carry_ref[...] = jnp.zeros(carry_ref.shape, F32)

    @pl.when(jnp.logical_not(first))
    def _():
        xbuf_ref[0:SUBLANES, :] = xbuf_ref[rows:rows + SUBLANES, :]

    xbuf_ref[SUBLANES:SUBLANES + rows, :] = x_ref[...]
    xc = cb_ref[...]
    for k in range(CONV_WIDTH):
        off = SUBLANES - (CONV_WIDTH - 1) + k
        xc = xc + cw_ref[k:k + 1, :] * xbuf_ref[off:off + rows, :]
    xcb = xc.astype(BF16)
    r = jax.nn.sigmoid(jnp.dot(xcb, wa_ref[...], preferred_element_type=F32) + ba_ref[...])
    i = jax.nn.sigmoid(jnp.dot(xcb, wx_ref[...], preferred_element_type=F32) + bx_ref[...])
    log_a = c1_ref[...] * r
    a_all = jnp.exp(log_a)
    b_all = jnp.sqrt(1.0 - a_all * a_all) * (i * xc)
    row_id = lax.broadcasted_iota(jnp.int32, (SUBLANES, x_ref.shape[1]), 0)
    carry = carry_ref[...]
    h_blocks = []
    for k in range(rows // SUBLANES):
        rs = slice(k * SUBLANES, (k + 1) * SUBLANES)
        a = a_all[rs]
        b = b_all[rs]
        for s in range(3):
            d = 1 << s
            m = row_id >= d
            sa = jnp.where(m, pltpu.roll(a, d, axis=0), 1.0)
            sb = jnp.where(m, pltpu.roll(b, d, axis=0), 0.0)
            a, b = a * sa, a * sb + b
        h = a * carry + b
        carry = _bcast_last_row(h)
        h_blocks.append(h)
    carry_ref[...] = carry
    o_ref[...] = jnp.concatenate(h_blocks, axis=0) * jax.nn.gelu(g_ref[...])


def _rglru(proj, x_block, gate_block, batch, seq, conv_w, conv_b, wa, ba, wx, bx, c1):
    n = proj.shape[0]
    bw = BRANCH_WIDTH
    steps = seq // SEQ_ROWS
    row_spec = lambda cb: pl.BlockSpec((SEQ_ROWS, bw), lambda b, l: (b * steps + l, cb))
    return pl.pallas_call(
        _rglru_body,
        grid=(batch, steps),
        in_specs=[
            row_spec(x_block), row_spec(gate_block),
            _const_spec((CONV_WIDTH, bw)), _const_spec((1, bw)),
            _const_spec((bw, bw)), _const_spec((1, bw)),
            _const_spec((bw, bw)), _const_spec((1, bw)),
            _const_spec((1, bw)),
        ],
        out_specs=pl.BlockSpec((SEQ_ROWS, bw), lambda b, l: (b * steps + l, 0)),
        out_shape=jax.ShapeDtypeStruct((n, bw), F32),
        scratch_shapes=[
            pltpu.VMEM((SEQ_ROWS + SUBLANES, bw), F32),
            pltpu.VMEM((SUBLANES, bw), F32),
        ],
        compiler_params=_params("parallel", "arbitrary"),
        name="rglru_mixer",
    )(proj, proj, conv_w, conv_b, wa, ba, wx, bx, c1)


def _rows_bcast(v, row):
    return jnp.broadcast_to(v[row:row + 1, :], (SUBLANES, v.shape[1]))


def _boundary_rows(b, m):
    c, w = b.shape
    if 2 * m >= SUBLANES:
        pieces = []
        for base in range(0, c, 2 * m):
            r = base + m - 1
            pieces.append(jnp.broadcast_to(b[r:r + 1, :], (2 * m, w)))
        return jnp.concatenate(pieces, axis=0)
    sub = lax.broadcasted_iota(jnp.int32, (SUBLANES, w), 0)
    pieces = []
    for base in range(0, c, SUBLANES):
        blk = b[base:base + SUBLANES, :]
        if m == 2:
            out = jnp.where(sub < 4, _rows_bcast(blk, 1), _rows_bcast(blk, 5))
        else:
            out = jnp.where(sub < 2, _rows_bcast(blk, 0),
                            jnp.where(sub < 4, _rows_bcast(blk, 2),
                                      jnp.where(sub < 6, _rows_bcast(blk, 4), _rows_bcast(blk, 6))))
        pieces.append(out)
    return jnp.concatenate(pieces, axis=0)


def _hgrn2_body(q_ref, f_ref, v_ref, g_ref, lb_ref, nw_ref, o_ref, st_ref):
    @pl.when(pl.program_id(1) == 0)
    def _():
        st_ref[...] = jnp.zeros(st_ref.shape, F32)

    rows = q_ref.shape[0]
    c = HG_CHUNK
    t_id = lax.broadcasted_iota(jnp.int32, (c, c), 0)
    s_id = lax.broadcasted_iota(jnp.int32, (c, c), 1)
    row_id = lax.broadcasted_iota(jnp.int32, (c, HG_DK), 0)
    levels = [1 << e for e in range(int(math.log2(c)))]
    same_block = {m: (t_id >> (e + 1)) == (s_id >> (e + 1)) for e, m in enumerate(levels)}
    right_half = {m: (row_id & m) != 0 for m in levels}
    tril = (s_id <= t_id).astype(F32)

    for hd in range(HG_HEADS):
        lanes = slice(hd * HG_DK, (hd + 1) * HG_DK)
        lb = lb_ref[:, lanes]
        st = st_ref[hd]
        for ck in range(rows // c):
            rs = slice(ck * c, (ck + 1) * c)
            qraw = q_ref[rs, lanes]
            z = f_ref[rs, lanes]
            v = v_ref[rs, lanes]
            q = qraw * jax.nn.sigmoid(qraw)
            logf = jnp.log(lb + (1.0 - lb) * jax.nn.sigmoid(z))
            k = (1.0 - lb) * jax.nn.sigmoid(-z)
            b = jnp.dot(tril, logf, preferred_element_type=F32, precision=lax.Precision.HIGHEST)
            vb = v.astype(BF16)
            scores = jnp.zeros((c, c), F32)
            for m in levels:
                bm = _boundary_rows(b, m)
                rh = right_half[m]
                w = jnp.exp(jnp.where(rh, b - bm, bm - b))
                qs = jnp.where(rh, q * w, 0.0).astype(BF16)
                ks = jnp.where(rh, 0.0, k * w).astype(BF16)
                part = lax.dot_general(qs, ks, (((1,), (1,)), ((), ())), preferred_element_type=F32)
                scores = scores + jnp.where(same_block[m], part, 0.0)
            o = jnp.dot(scores.astype(BF16), vb, preferred_element_type=F32)
            o = o + jnp.sum(q * k, axis=-1, keepdims=True) * v
            qd = (q * jnp.exp(b)).astype(BF16)
            o = o + lax.dot_general(qd, st.astype(BF16), (((1,), (1,)), ((), ())),
                                    preferred_element_type=F32)
            b_last = b[c - 1:c, :]
            kd = (k * jnp.exp(b_last - b)).astype(BF16)
            st = st * jnp.exp(b_last) + lax.dot_general(
                vb, kd, (((0,), (0,)), ((), ())), preferred_element_type=F32)
            o = o * lax.rsqrt(jnp.mean(o * o, axis=-1, keepdims=True) + EPS) * nw_ref[:, lanes]
            graw = g_ref[rs, lanes]
            o_ref[rs, lanes] = o * (graw * jax.nn.sigmoid(graw))
        st_ref[hd] = st


def _hgrn2(proj, q_block, f_block, v_block, g_block, batch, seq, lb, norm_w):
    n = proj.shape[0]
    bw = BRANCH_WIDTH
    steps = seq // SEQ_ROWS
    row_spec = lambda cb: pl.BlockSpec((SEQ_ROWS, bw), lambda b, l: (b * steps + l, cb))
    return pl.pallas_call(
        _hgrn2_body,
        grid=(batch, steps),
        in_specs=[row_spec(q_block), row_spec(f_block), row_spec(v_block), row_spec(g_block),
                  _const_spec((1, bw)), _const_spec((1, bw))],
        out_specs=pl.BlockSpec((SEQ_ROWS, bw), lambda b, l: (b * steps + l, 0)),
        out_shape=jax.ShapeDtypeStruct((n, bw), F32),
        scratch_shapes=[pltpu.VMEM((HG_HEADS, HG_DV, HG_DK), F32)],
        compiler_params=_params("parallel", "arbitrary"),
        name="hgrn2_mixer",
    )(proj, proj, proj, proj, lb, norm_w)


def _merge_body(x_ref, ya_ref, yb_ref, yc_ref, ga_ref, gb_ref, gc_ref, p_ref, wo_ref, o_ref):
    merged = None
    for n, (y_ref, g_ref) in enumerate(((ya_ref, ga_ref), (yb_ref, gb_ref), (yc_ref, gc_ref))):
        up = jnp.dot(y_ref[...].astype(BF16), p_ref[n], preferred_element_type=F32)
        term = jax.nn.sigmoid(g_ref[...]) * up
        merged = term if merged is None else merged + term
    o_ref[...] = x_ref[...] + jnp.dot(merged.astype(BF16), wo_ref[...], preferred_element_type=F32)


def _merge(x, ya, yb, yc, proj, gate_blocks, p, wo):
    n, d = x.shape
    bw = BRANCH_WIDTH
    y_spec = pl.BlockSpec((MERGE_ROWS, bw), lambda i: (i, 0))
    g_spec = lambda cb: pl.BlockSpec((MERGE_ROWS, d), lambda i: (i, cb))
    return pl.pallas_call(
        _merge_body,
        grid=(n // MERGE_ROWS,),
        in_specs=[pl.BlockSpec((MERGE_ROWS, d), lambda i: (i, 0)), y_spec, y_spec, y_spec,
                  g_spec(gate_blocks[0]), g_spec(gate_blocks[1]), g_spec(gate_blocks[2]),
                  _const_spec((N_BRANCH, bw, d)), _const_spec((d, d))],
        out_specs=pl.BlockSpec((MERGE_ROWS, d), lambda i: (i, 0)),
        out_shape=jax.ShapeDtypeStruct((n, d), F32),
        compiler_params=_params("parallel"),
        name="merge_out_proj",
    )(x, ya, yb, yc, proj, proj, proj, p, wo)


def _s5_tables(lam_re, lam_im, log_dt, b_re, b_im, c_re, c_im):
    groups, states = lam_re.shape
    n_tiles = groups // S5_GROUPS_PER_TILE
    lr = jnp.minimum(lam_re.astype(F32), S5_EIG_MAX)
    li = lam_im.astype(F32)
    dt = jnp.exp(log_dt.astype(F32))[:, None]
    mag = jnp.exp(lr * dt)
    ar = mag * jnp.cos(li * dt)
    ai = mag * jnp.sin(li * dt)
    den = lr * lr + li * li
    fr = ((ar - 1.0) * lr + ai * li) / den
    fi = (ai * lr - (ar - 1.0) * li) / den
    br, bi = b_re.astype(F32), b_im.astype(F32)
    bbr = fr[..., None] * br - fi[..., None] * bi
    bbi = fr[..., None] * bi + fi[..., None] * br
    eye = jnp.eye(S5_GROUPS_PER_TILE, dtype=F32)

    def in_tiles(m):
        m = m.reshape(n_tiles, S5_GROUPS_PER_TILE, states, S5_GROUP)
        t = jnp.einsum('jgpc,gh->jgchp', m, eye)
        return t.reshape(n_tiles, LANES, S5_TILE_STATES)

    def out_tiles(m):
        m = m.reshape(n_tiles, S5_GROUPS_PER_TILE, S5_GROUP, states)
        t = jnp.einsum('jgcp,gh->jgphc', m, eye)
        return t.reshape(n_tiles, S5_TILE_STATES, LANES)

    bmat = jnp.concatenate([in_tiles(bbr), in_tiles(bbi)], axis=2).astype(BF16)
    cmat = jnp.concatenate([out_tiles(c_re.astype(F32)), -out_tiles(c_im.astype(F32))],
                           axis=1).astype(BF16)

    def power(kk):
        m = jnp.exp(lr * dt * kk)
        return m * jnp.cos(li * dt * kk), m * jnp.sin(li * dt * kk)

    rows = jnp.arange(SUBLANES, dtype=F32)
    tre, tim = [], []
    for s in range(3):
        d = float(1 << s)
        pr, pi = power(jnp.full((SUBLANES, 1, 1), d, F32))
        keep = (rows >= d)[:, None, None]
        tre.append(jnp.where(keep, pr, 0.0))
        tim.append(jnp.where(keep, pi, 0.0))
    pr, pi = power((rows + 1.0)[:, None, None])
    tre.append(pr)
    tim.append(pi)
    tre = jnp.stack(tre).reshape(4, SUBLANES, groups * states)
    tim = jnp.stack(tim).reshape(4, SUBLANES, groups * states)
    return {"bmat": bmat, "cmat": cmat, "tre": tre, "tim": tim}


def _block_diag(w):
    nb, bi, bo = w.shape
    eye = jnp.eye(nb, dtype=w.dtype)
    return jnp.einsum('hij,hk->hikj', w, eye).reshape(nb * bi, nb * bo)


def _permute_w_in(w_in, d_model):
    mix = 7 * BRANCH_WIDTH
    return jnp.concatenate([w_in[:, mix:], w_in[:, :mix]], axis=1)


def kernel(x, norm_w, final_norm_w, ffn_gate, ffn_up, ffn_down, w_in, branch_proj, w_out, s5_lambda_re, s5_lambda_im, s5_log_dt, s5_b_re, s5_b_im, s5_c_re, s5_c_im, s5_d, s5_glu_w, s5_glu_b, hg_lb_logits, hg_norm_w, rg_conv_w, rg_conv_b, rg_wa, rg_ba, rg_wx, rg_bx, rg_lambda):
    batch, seq, d = x.shape
    depth = norm_w.shape[0]
    d_ff = ffn_gate.shape[-1]
    nf = d_ff // FFN_FCHUNK
    bw = BRANCH_WIDTH
    gate_cols = N_BRANCH * d // bw
    col = {name: gate_cols + k for k, name in enumerate(("u", "q", "f", "v", "g", "xc", "gc"))}

    p = jax.nn.softmax(hg_lb_logits.astype(F32), axis=0)
    lower_bounds = jnp.cumsum(p, axis=0) - p[0]

    h = x.reshape(batch * seq, d)
    row = lambda v: v.reshape(1, -1).astype(F32)
    for l in range(depth):
        def ffn_w(j):
            wg = ffn_gate[l, j].astype(BF16).reshape(d, nf, FFN_FCHUNK).transpose(1, 0, 2)
            wu = ffn_up[l, j].astype(BF16).reshape(d, nf, FFN_FCHUNK).transpose(1, 0, 2)
            wd = ffn_down[l, j].astype(BF16).reshape(nf, FFN_FCHUNK, d)
            return wg, wu, wd

        h = _ffn(h, row(norm_w[l, 0]), *ffn_w(0))
        proj = _proj(h, row(norm_w[l, 1]), _permute_w_in(w_in[l], d).astype(BF16))
        tabs = _s5_tables(s5_lambda_re[l], s5_lambda_im[l], s5_log_dt[l], s5_b_re[l], s5_b_im[l],
                          s5_c_re[l], s5_c_im[l])
        ya = _s5(proj, col["u"], batch, seq, tabs, row(s5_d[l]), s5_glu_w[l].astype(BF16),
                 row(s5_glu_b[l]))
        yb = _hgrn2(proj, col["q"], col["f"], col["v"], col["g"], batch, seq,
                    row(lower_bounds[l]), row(hg_norm_w[l]))
        c1 = -RG_C * jax.nn.softplus(-rg_lambda[l].astype(F32))
        yc = _rglru(proj, col["xc"], col["gc"], batch, seq, rg_conv_w[l].astype(F32),
                    row(rg_conv_b[l]), _block_diag(rg_wa[l]).astype(BF16), row(rg_ba[l]),
                    _block_diag(rg_wx[l]).astype(BF16), row(rg_bx[l]), row(c1))
        h = _merge(h, ya, yb, yc, proj, (0, 1, 2), branch_proj[l].astype(BF16),
                   w_out[l].astype(BF16))
        last = l == depth - 1
        h = _ffn(h, row(norm_w[l, 2]), *ffn_w(1), final_w=row(final_norm_w) if last else None)
    return h.reshape(batch, seq, d)
```

```python
import functools
import math

import jax
import jax.numpy as jnp
from jax import lax
from jax.experimental import pallas as pl
from jax.experimental.pallas import tpu as pltpu

F32 = jnp.float32
BF16 = jnp.bfloat16

LANES = 128
SUBLANES = 8
LOG2_SUBLANES = 3
MXU_WIDTH = 256
VMEM_LIMIT_BYTES = 56 * 1024 * 1024

SCAN_CARRY = LOG2_SUBLANES
SCAN_TABLES = LOG2_SUBLANES + 1

EPS = 1e-6
BRANCH_WIDTH = 512
N_BRANCH = 3
S5_GROUP = 16
S5_STATE = 64
S5_EIG_MAX = -1e-4
S5_GROUPS_PER_TILE = LANES // S5_GROUP
S5_TILE_STATES = S5_GROUPS_PER_TILE * S5_STATE
S5_T = 16
S5_COL_BLOCK = MXU_WIDTH
HG_HEADS = 4
HG_DK = 128
HG_DV = 128
HG_CHUNK = 64
RG_BLOCKS = 8
RG_BLOCK = 64
RG_C = 8.0
CONV_WIDTH = 4

FFN_ROWS = 512
FFN_FCHUNK = 256
PROJ_ROWS = 1024
PROJ_COLS = 1664
SEQ_ROWS = 256
MERGE_ROWS = 512


def _rmsnorm(x, w):
    ms = jnp.mean(x * x, axis=-1, keepdims=True)
    return x * lax.rsqrt(ms + EPS) * w


def _const_spec(shape, single_buffer=False):
    nd = len(shape)
    kw = {"pipeline_mode": pl.Buffered(1)} if single_buffer else {}
    return pl.BlockSpec(shape, lambda *_: (0,) * nd, **kw)


def _params(*sem):
    return pltpu.CompilerParams(dimension_semantics=sem, vmem_limit_bytes=VMEM_LIMIT_BYTES)


def _ffn_body(x_ref, nw_ref, wg_ref, wu_ref, wd_ref, *rest, final):
    o_ref = rest[-1]
    x = x_ref[...]
    h = _rmsnorm(x, nw_ref[...]).astype(BF16)
    acc = jnp.zeros(x.shape, F32)
    for c in range(wg_ref.shape[1] // FFN_FCHUNK):
        cols = slice(c * FFN_FCHUNK, (c + 1) * FFN_FCHUNK)
        g = jnp.dot(h, wg_ref[:, cols], preferred_element_type=F32)
        u = jnp.dot(h, wu_ref[:, cols], preferred_element_type=F32)
        a = (g * jax.nn.sigmoid(g) * u).astype(BF16)
        acc = acc + jnp.dot(a, wd_ref[cols, :], preferred_element_type=F32)
    y = x + 0.5 * acc
    if final:
        y = _rmsnorm(y, rest[0][...])
    o_ref[...] = y


def _ffn(x, nw, wg, wu, wd, final_w=None):
    n, d = x.shape
    d_ff = wg.shape[1]
    final = final_w is not None
    in_specs = [
        pl.BlockSpec((FFN_ROWS, d), lambda i: (i, 0)),
        _const_spec((1, d)),
        _const_spec((d, d_ff), single_buffer=True),
        _const_spec((d, d_ff), single_buffer=True),
        _const_spec((d_ff, d), single_buffer=True),
    ]
    args = [x, nw, wg, wu, wd]
    if final:
        in_specs.append(_const_spec((1, d)))
        args.append(final_w)
    return pl.pallas_call(
        functools.partial(_ffn_body, final=final),
        grid=(n // FFN_ROWS,),
        in_specs=in_specs,
        out_specs=pl.BlockSpec((FFN_ROWS, d), lambda i: (i, 0)),
        out_shape=jax.ShapeDtypeStruct((n, d), F32),
        compiler_params=_params("parallel"),
        name="ffn_final" if final else "ffn",
    )(*args)


def _proj_body(x_ref, nw_ref, w_ref, o_ref, h_ref):
    @pl.when(pl.program_id(1) == 0)
    def _():
        h_ref[...] = _rmsnorm(x_ref[...], nw_ref[...]).astype(BF16)

    o_ref[...] = jnp.dot(h_ref[...], w_ref[...], preferred_element_type=F32)


def _proj(x, nw, w):
    n, d = x.shape
    cols = w.shape[1]
    return pl.pallas_call(
        _proj_body,
        grid=(n // PROJ_ROWS, cols // PROJ_COLS),
        in_specs=[
            pl.BlockSpec((PROJ_ROWS, d), lambda i, j: (i, 0)),
            _const_spec((1, d)),
            pl.BlockSpec((d, PROJ_COLS), lambda i, j: (0, j)),
        ],
        out_specs=pl.BlockSpec((PROJ_ROWS, PROJ_COLS), lambda i, j: (i, j)),
        out_shape=jax.ShapeDtypeStruct((n, cols), F32),
        scratch_shapes=[pltpu.VMEM((PROJ_ROWS, d), BF16)],
        compiler_params=_params("parallel", "arbitrary"),
        name="mixer_in_proj",
    )(x, nw, w)


def _bcast_last_row(v):
    return jnp.broadcast_to(v[SUBLANES - 1:SUBLANES, :], v.shape)


def _s5_body(u_ref, mt_ref, g_ref, h_ref, tre_ref, tim_ref, d_ref, o_ref):
    seq = u_ref.shape[0]
    nch = seq // S5_T
    ts = S5_TILE_STATES
    pieces = [u_ref[pl.ds(t, nch, stride=S5_T), :] for t in range(S5_T)]
    ucat = jnp.concatenate(pieces, axis=1).astype(BF16)
    x_end = jnp.dot(ucat, g_ref[0], preferred_element_type=F32)

    row_id = lax.broadcasted_iota(jnp.int32, (SUBLANES, ts), 0)
    cr = jnp.zeros((SUBLANES, ts), F32)
    ci = jnp.zeros((SUBLANES, ts), F32)
    pr = tre_ref[SCAN_CARRY]
    pi = tim_ref[SCAN_CARRY]
    prev_r, prev_i = [], []
    for k in range(nch // SUBLANES):
        rs = slice(k * SUBLANES, (k + 1) * SUBLANES)
        xr = x_end[rs, :ts]
        xi = x_end[rs, ts:]
        for s in range(LOG2_SUBLANES):
            ar = tre_ref[s]
            ai = tim_ref[s]
            sr = pltpu.roll(xr, 1 << s, axis=0)
            si = pltpu.roll(xi, 1 << s, axis=0)
            xr, xi = xr + ar * sr - ai * si, xi + ar * si + ai * sr
        xr, xi = xr + pr * cr - pi * ci, xi + pr * ci + pi * cr
        prev_r.append(jnp.where(row_id == 0, cr, pltpu.roll(xr, 1, axis=0)))
        prev_i.append(jnp.where(row_id == 0, ci, pltpu.roll(xi, 1, axis=0)))
        cr = _bcast_last_row(xr)
        ci = _bcast_last_row(xi)
    xprev = jnp.concatenate(
        [jnp.concatenate(prev_r, axis=0), jnp.concatenate(prev_i, axis=0)], axis=1).astype(BF16)
    y_carry = jnp.dot(xprev, h_ref[0], preferred_element_type=F32)

    steps_per_block = S5_COL_BLOCK // LANES
    for c in range(S5_T // steps_per_block):
        cols = slice(c * S5_COL_BLOCK, (c + 1) * S5_COL_BLOCK)
        k_hi = (c + 1) * S5_COL_BLOCK
        y_c = y_carry[:, cols] + jnp.dot(ucat[:, :k_hi], mt_ref[0, :k_hi, cols],
                                         preferred_element_type=F32)
        for tt in range(steps_per_block):
            t = c * steps_per_block + tt
            y_t = y_c[:, tt * LANES:(tt + 1) * LANES] + d_ref[...] * pieces[t]
            o_ref[pl.ds(t, nch, stride=S5_T), :] = jax.nn.gelu(y_t)


def _s5(proj, col_block, batch, seq, tabs, d_skip):
    n = proj.shape[0]
    n_tiles = BRANCH_WIDTH // LANES
    tl = S5_T * LANES
    ts2 = 2 * S5_TILE_STATES
    return pl.pallas_call(
        _s5_body,
        grid=(n_tiles, batch),
        in_specs=[
            pl.BlockSpec((seq, LANES), lambda j, b: (b, col_block * n_tiles + j)),
            pl.BlockSpec((1, tl, tl), lambda j, b: (j, 0, 0)),
            pl.BlockSpec((1, tl, ts2), lambda j, b: (j, 0, 0)),
            pl.BlockSpec((1, ts2, tl), lambda j, b: (j, 0, 0)),
            pl.BlockSpec((SCAN_TABLES, SUBLANES, S5_TILE_STATES), lambda j, b: (0, 0, j)),
            pl.BlockSpec((SCAN_TABLES, SUBLANES, S5_TILE_STATES), lambda j, b: (0, 0, j)),
            pl.BlockSpec((1, LANES), lambda j, b: (0, j)),
        ],
        out_specs=pl.BlockSpec((seq, LANES), lambda j, b: (b, j)),
        out_shape=jax.ShapeDtypeStruct((n, BRANCH_WIDTH), F32),
        compiler_params=_params("parallel", "parallel"),
        name="s5_mixer",
    )(proj, tabs["mt"], tabs["g"], tabs["h"], tabs["tre"], tabs["tim"], d_skip)


def _rglru_body(x_ref, g_ref, cw_ref, cb_ref, wa_ref, ba_ref, wx_ref, bx_ref, c1_ref, o_ref,
                xbuf_ref, carry_ref):
    rows = x_ref.shape[0]
    first = pl.program_id(1) == 0

    @pl.when(first)
    def _():
        xbuf_ref[0:SUBLANES, :] = jnp.zeros((SUBLANES, x_ref.shape[1]), F32)
        carry_ref[...] = jnp.zeros(carry_ref.shape, F32)

    @pl.when(jnp.logical_not(first))
    def _():
        xbuf_ref[0:SUBLANES, :] = xbuf_ref[rows:rows + SUBLANES, :]

    xbuf_ref[SUBLANES:SUBLANES + rows, :] = x_ref[...]
    xc = cb_ref[...]
    for k in range(CONV_WIDTH):
        off = SUBLANES - (CONV_WIDTH - 1) + k
        xc = xc + cw_ref[k:k + 1, :] * xbuf_ref[off:off + rows, :]
    xcb = xc.astype(BF16)
    r = jax.nn.sigmoid(jnp.dot(xcb, wa_ref[...], preferred_element_type=F32) + ba_ref[...])
    i = jax.nn.sigmoid(jnp.dot(xcb, wx_ref[...], preferred_element_type=F32) + bx_ref[...])
    log_a = c1_ref[...] * r
    a_all = jnp.exp(log_a)
    b_all = jnp.sqrt(1.0 - a_all * a_all) * (i * xc)
    row_id = lax.broadcasted_iota(jnp.int32, (SUBLANES, x_ref.shape[1]), 0)
    carry = carry_ref[...]
    h_blocks = []
    for k in range(rows // SUBLANES):
        rs = slice(k * SUBLANES, (k + 1) * SUBLANES)
        a = a_all[rs]
        b = b_all[rs]
        for s in range(LOG2_SUBLANES):
            d = 1 << s
            m = row_id >= d
            sa = jnp.where(m, pltpu.roll(a, d, axis=0), 1.0)
            sb = jnp.where(m, pltpu.roll(b, d, axis=0), 0.0)
            a, b = a * sa, a * sb + b
        h = a * carry + b
        carry = _bcast_last_row(h)
        h_blocks.append(h)
    carry_ref[...] = carry
    o_ref[...] = jnp.concatenate(h_blocks, axis=0) * jax.nn.gelu(g_ref[...])


def _rglru(proj, x_block, gate_block, batch, seq, conv_w, conv_b, wa, ba, wx, bx, c1):
    n = proj.shape[0]
    bw = BRANCH_WIDTH
    steps = seq // SEQ_ROWS
    row_spec = lambda cb: pl.BlockSpec((SEQ_ROWS, bw), lambda b, l: (b * steps + l, cb))
    return pl.pallas_call(
        _rglru_body,
        grid=(batch, steps),
        in_specs=[
            row_spec(x_block), row_spec(gate_block),
            _const_spec((CONV_WIDTH, bw)), _const_spec((1, bw)),
            _const_spec((bw, bw)), _const_spec((1, bw)),
            _const_spec((bw, bw)), _const_spec((1, bw)),
            _const_spec((1, bw)),
        ],
        out_specs=pl.BlockSpec((SEQ_ROWS, bw), lambda b, l: (b * steps + l, 0)),
        out_shape=jax.ShapeDtypeStruct((n, bw), F32),
        scratch_shapes=[
            pltpu.VMEM((SEQ_ROWS + SUBLANES, bw), F32),
            pltpu.VMEM((SUBLANES, bw), F32),
        ],
        compiler_params=_params("parallel", "arbitrary"),
        name="rglru_mixer",
    )(proj, proj, conv_w, conv_b, wa, ba, wx, bx, c1)


def _rows_bcast(v, row):
    return jnp.broadcast_to(v[row:row + 1, :], (SUBLANES, v.shape[1]))


def _boundary_rows(b, m):
    c, w = b.shape
    if 2 * m >= SUBLANES:
        pieces = []
        for base in range(0, c, 2 * m):
            r = base + m - 1
            pieces.append(jnp.broadcast_to(b[r:r + 1, :], (2 * m, w)))
        return jnp.concatenate(pieces, axis=0)
    sub = lax.broadcasted_iota(jnp.int32, (SUBLANES, w), 0)
    pieces = []
    for base in range(0, c, SUBLANES):
        blk = b[base:base + SUBLANES, :]
        if m == 2:
            out = jnp.where(sub < 4, _rows_bcast(blk, 1), _rows_bcast(blk, 5))
        else:
            out = jnp.where(sub < 2, _rows_bcast(blk, 0),
                            jnp.where(sub < 4, _rows_bcast(blk, 2),
                                      jnp.where(sub < 6, _rows_bcast(blk, 4), _rows_bcast(blk, 6))))
        pieces.append(out)
    return jnp.concatenate(pieces, axis=0)


def _hgrn2_body(q_ref, f_ref, v_ref, g_ref, lb_ref, nw_ref, o_ref, st_ref):
    @pl.when(pl.program_id(1) == 0)
    def _():
        st_ref[...] = jnp.zeros(st_ref.shape, F32)

    rows = q_ref.shape[0]
    c = HG_CHUNK
    t_id = lax.broadcasted_iota(jnp.int32, (c, c), 0)
    s_id = lax.broadcasted_iota(jnp.int32, (c, c), 1)
    row_id = lax.broadcasted_iota(jnp.int32, (c, HG_DK), 0)
    levels = [1 << e for e in range(int(math.log2(c)))]
    same_block = {m: (t_id >> (e + 1)) == (s_id >> (e + 1)) for e, m in enumerate(levels)}
    right_half = {m: (row_id & m) != 0 for m in levels}
    tril = (s_id <= t_id).astype(F32)

    for hd in range(HG_HEADS):
        lanes = slice(hd * HG_DK, (hd + 1) * HG_DK)
        lb = lb_ref[:, lanes]
        st = st_ref[hd]
        for ck in range(rows // c):
            rs = slice(ck * c, (ck + 1) * c)
            qraw = q_ref[rs, lanes]
            z = f_ref[rs, lanes]
            v = v_ref[rs, lanes]
            q = qraw * jax.nn.sigmoid(qraw)
            logf = jnp.log(lb + (1.0 - lb) * jax.nn.sigmoid(z))
            k = (1.0 - lb) * jax.nn.sigmoid(-z)
            b = jnp.dot(tril, logf, preferred_element_type=F32, precision=lax.Precision.HIGHEST)
            vb = v.astype(BF16)
            scores = jnp.zeros((c, c), F32)
            for m in levels:
                bm = _boundary_rows(b, m)
                rh = right_half[m]
                w = jnp.exp(jnp.where(rh, b - bm, bm - b))
                qs = jnp.where(rh, q * w, 0.0).astype(BF16)
                ks = jnp.where(rh, 0.0, k * w).astype(BF16)
                part = lax.dot_general(qs, ks, (((1,), (1,)), ((), ())), preferred_element_type=F32)
                scores = scores + jnp.where(same_block[m], part, 0.0)
            o = jnp.dot(scores.astype(BF16), vb, preferred_element_type=F32)
            o = o + jnp.sum(q * k, axis=-1, keepdims=True) * v
            qd = (q * jnp.exp(b)).astype(BF16)
            o = o + lax.dot_general(qd, st.astype(BF16), (((1,), (1,)), ((), ())),
                                    preferred_element_type=F32)
            b_last = b[c - 1:c, :]
            kd = (k * jnp.exp(b_last - b)).astype(BF16)
            st = st * jnp.exp(b_last) + lax.dot_general(
                vb, kd, (((0,), (0,)), ((), ())), preferred_element_type=F32)
            o = o * lax.rsqrt(jnp.mean(o * o, axis=-1, keepdims=True) + EPS) * nw_ref[:, lanes]
            graw = g_ref[rs, lanes]
            o_ref[rs, lanes] = o * (graw * jax.nn.sigmoid(graw))
        st_ref[hd] = st


def _hgrn2(proj, q_block, f_block, v_block, g_block, batch, seq, lb, norm_w):
    n = proj.shape[0]
    bw = BRANCH_WIDTH
    steps = seq // SEQ_ROWS
    row_spec = lambda cb: pl.BlockSpec((SEQ_ROWS, bw), lambda b, l: (b * steps + l, cb))
    return pl.pallas_call(
        _hgrn2_body,
        grid=(batch, steps),
        in_specs=[row_spec(q_block), row_spec(f_block), row_spec(v_block), row_spec(g_block),
                  _const_spec((1, bw)), _const_spec((1, bw))],
        out_specs=pl.BlockSpec((SEQ_ROWS, bw), lambda b, l: (b * steps + l, 0)),
        out_shape=jax.ShapeDtypeStruct((n, bw), F32),
        scratch_shapes=[pltpu.VMEM((HG_HEADS, HG_DV, HG_DK), F32)],
        compiler_params=_params("parallel", "arbitrary"),
        name="hgrn2_mixer",
    )(proj, proj, proj, proj, lb, norm_w)


def _merge_body(x_ref, za_ref, yb_ref, yc_ref, *rest):
    gate_refs, (gw_ref, gb_ref, p_ref, wo_ref, o_ref) = rest[:-5], rest[-5:]
    halves = len(gate_refs) // N_BRANCH
    za = za_ref[...]
    glu = jnp.dot(za.astype(BF16), gw_ref[...], preferred_element_type=F32) + gb_ref[...]
    ya = za * jax.nn.sigmoid(glu)
    merged = None
    for n, y in enumerate((ya, yb_ref[...], yc_ref[...])):
        up = jnp.dot(y.astype(BF16), p_ref[n], preferred_element_type=F32)
        gate = jnp.concatenate([g[...] for g in gate_refs[n * halves:(n + 1) * halves]], axis=1)
        term = jax.nn.sigmoid(gate) * up
        merged = term if merged is None else merged + term
    o_ref[...] = x_ref[...] + jnp.dot(merged.astype(BF16), wo_ref[...], preferred_element_type=F32)


def _merge(x, za, yb, yc, proj, first_gate_block, glu_w, glu_b, p, wo):
    n, d = x.shape
    bw = BRANCH_WIDTH
    n_gate_blocks = N_BRANCH * d // bw
    y_spec = pl.BlockSpec((MERGE_ROWS, bw), lambda i: (i, 0))
    g_spec = lambda cb: pl.BlockSpec((MERGE_ROWS, bw), lambda i: (i, cb))
    return pl.pallas_call(
        _merge_body,
        grid=(n // MERGE_ROWS,),
        in_specs=[pl.BlockSpec((MERGE_ROWS, d), lambda i: (i, 0)), y_spec, y_spec, y_spec]
        + [g_spec(first_gate_block + k) for k in range(n_gate_blocks)]
        + [_const_spec((bw, bw)), _const_spec((1, bw)),
           _const_spec((N_BRANCH, bw, d)), _const_spec((d, d))],
        out_specs=pl.BlockSpec((MERGE_ROWS, d), lambda i: (i, 0)),
        out_shape=jax.ShapeDtypeStruct((n, d), F32),
        compiler_params=_params("parallel"),
        name="merge_out_proj",
    )(x, za, yb, yc, *([proj] * n_gate_blocks), glu_w, glu_b, p, wo)


def _s5_tables(lam_re, lam_im, log_dt, b_re, b_im, c_re, c_im):
    groups, states = lam_re.shape
    n_tiles = groups // S5_GROUPS_PER_TILE
    lr = jnp.minimum(lam_re.astype(F32), S5_EIG_MAX)
    li = lam_im.astype(F32)
    dt = jnp.exp(log_dt.astype(F32))[:, None]
    mag = jnp.exp(lr * dt)
    ar = mag * jnp.cos(li * dt)
    ai = mag * jnp.sin(li * dt)
    den = lr * lr + li * li
    fr = ((ar - 1.0) * lr + ai * li) / den
    fi = (ai * lr - (ar - 1.0) * li) / den
    br, bi = b_re.astype(F32), b_im.astype(F32)
    bbr = fr[..., None] * br - fi[..., None] * bi
    bbi = fr[..., None] * bi + fi[..., None] * br
    cr, ci = c_re.astype(F32), c_im.astype(F32)
    gpt = S5_GROUPS_PER_TILE
    eye = jnp.eye(gpt, dtype=F32)
    tl = S5_T * LANES

    def power(kk):
        m = jnp.exp(lr * dt * kk)
        return m * jnp.cos(li * dt * kk), m * jnp.sin(li * dt * kk)

    taus = jnp.arange(S5_T, dtype=F32)
    pr, pi = power(taus[:, None, None])
    wr = pr[..., None] * bbr - pi[..., None] * bbi
    wi = pr[..., None] * bbi + pi[..., None] * bbr

    kt = jnp.einsum('gop,tgpi->tgio', cr, wr) - jnp.einsum('gop,tgpi->tgio', ci, wi)
    ti = jnp.arange(S5_T)
    lag = ti[None, :] - ti[:, None]
    ktoe = jnp.where((lag >= 0)[:, :, None, None, None], kt[jnp.maximum(lag, 0)], 0.0)
    ktoe = ktoe.reshape(S5_T, S5_T, n_tiles, gpt, S5_GROUP, S5_GROUP)
    mt = jnp.einsum('abjgio,gh->jagibho', ktoe, eye).reshape(n_tiles, tl, tl)

    def state_in(w):
        w = w[::-1].reshape(S5_T, n_tiles, gpt, states, S5_GROUP)
        return jnp.einsum('ajgpi,gh->jagihp', w, eye).reshape(n_tiles, tl, S5_TILE_STATES)

    gmat = jnp.concatenate([state_in(wr), state_in(wi)], axis=2)

    pr1, pi1 = power(taus[:, None, None] + 1.0)
    hr = cr[None] * pr1[:, :, None, :] - ci[None] * pi1[:, :, None, :]
    hi = -(cr[None] * pi1[:, :, None, :] + ci[None] * pr1[:, :, None, :])

    def state_out(hh):
        hh = hh.reshape(S5_T, n_tiles, gpt, S5_GROUP, states)
        return jnp.einsum('bjgop,gh->jgpbho', hh, eye).reshape(n_tiles, S5_TILE_STATES, tl)

    hmat = jnp.concatenate([state_out(hr), state_out(hi)], axis=1)

    rows = jnp.arange(SUBLANES, dtype=F32)
    tre, tim = [], []
    for s in range(LOG2_SUBLANES):
        d = float(1 << s)
        sr, si = power(jnp.full((SUBLANES, 1, 1), d * S5_T, F32))
        keep = (rows >= d)[:, None, None]
        tre.append(jnp.where(keep, sr, 0.0))
        tim.append(jnp.where(keep, si, 0.0))
    sr, si = power((rows[:, None, None] + 1.0) * S5_T)
    tre.append(sr)
    tim.append(si)
    tre = jnp.stack(tre).reshape(SCAN_TABLES, SUBLANES, groups * states)
    tim = jnp.stack(tim).reshape(SCAN_TABLES, SUBLANES, groups * states)
    return {"mt": mt.astype(BF16), "g": gmat.astype(BF16), "h": hmat.astype(BF16),
            "tre": tre, "tim": tim}


def _block_diag(w):
    nb, bi, bo = w.shape
    eye = jnp.eye(nb, dtype=w.dtype)
    return jnp.einsum('hij,hk->hikj', w, eye).reshape(nb * bi, nb * bo)


def kernel(x, norm_w, final_norm_w, ffn_gate, ffn_up, ffn_down, w_in, branch_proj, w_out, s5_lambda_re, s5_lambda_im, s5_log_dt, s5_b_re, s5_b_im, s5_c_re, s5_c_im, s5_d, s5_glu_w, s5_glu_b, hg_lb_logits, hg_norm_w, rg_conv_w, rg_conv_b, rg_wa, rg_ba, rg_wx, rg_bx, rg_lambda):
    batch, seq, d = x.shape
    depth = norm_w.shape[0]
    col = {name: k for k, name in enumerate(("u", "q", "f", "v", "g", "xc", "gc", "merge"))}

    p = jax.nn.softmax(hg_lb_logits.astype(F32), axis=0)
    lower_bounds = jnp.cumsum(p, axis=0) - p[0]

    h = x.reshape(batch * seq, d)
    row = lambda v: v.reshape(1, -1).astype(F32)
    for l in range(depth):
        def ffn_w(j):
            return (ffn_gate[l, j].astype(BF16), ffn_up[l, j].astype(BF16),
                    ffn_down[l, j].astype(BF16))

        h = _ffn(h, row(norm_w[l, 0]), *ffn_w(0))
        proj = _proj(h, row(norm_w[l, 1]), w_in[l].astype(BF16))
        tabs = _s5_tables(s5_lambda_re[l], s5_lambda_im[l], s5_log_dt[l], s5_b_re[l], s5_b_im[l],
                          s5_c_re[l], s5_c_im[l])
        za = _s5(proj, col["u"], batch, seq, tabs, row(s5_d[l]))
        yb = _hgrn2(proj, col["q"], col["f"], col["v"], col["g"], batch, seq,
                    row(lower_bounds[l]), row(hg_norm_w[l]))
        c1 = -RG_C * jax.nn.softplus(-rg_lambda[l].astype(F32))
        yc = _rglru(proj, col["xc"], col["gc"], batch, seq, rg_conv_w[l].astype(F32),
                    row(rg_conv_b[l]), _block_diag(rg_wa[l]).astype(BF16), row(rg_ba[l]),
                    _block_diag(rg_wx[l]).astype(BF16), row(rg_bx[l]), row(c1))
        h = _merge(h, za, yb, yc, proj, col["merge"], s5_glu_w[l].astype(BF16),
                   row(s5_glu_b[l]), branch_proj[l].astype(BF16),
                   w_out[l].astype(BF16))
        last = l == depth - 1
        h = _ffn(h, row(norm_w[l, 2]), *ffn_w(1), final_w=row(final_norm_w) if last else None)
    return h.reshape(batch, seq, d)
```

```python
import functools
import math

import jax
import jax.numpy as jnp
from jax import lax
from jax.experimental import pallas as pl
from jax.experimental.pallas import tpu as pltpu

F32 = jnp.float32
BF16 = jnp.bfloat16

LANES = 128
SUBLANES = 8
LOG2_SUBLANES = 3
MXU_WIDTH = 256
VMEM_LIMIT_BYTES = 56 * 1024 * 1024

SCAN_CARRY = LOG2_SUBLANES
SCAN_TABLES = LOG2_SUBLANES + 1

EPS = 1e-6
BRANCH_WIDTH = 512
N_BRANCH = 3
S5_GROUP = 16
S5_STATE = 64
S5_EIG_MAX = -1e-4
S5_GROUPS_PER_TILE = LANES // S5_GROUP
S5_TILE_STATES = S5_GROUPS_PER_TILE * S5_STATE
S5_T = 16
S5_COL_BLOCK = MXU_WIDTH
HG_HEADS = 4
HG_DK = 128
HG_DV = 128
HG_CHUNK = 64
RG_BLOCKS = 8
RG_BLOCK = 64
RG_C = 8.0
CONV_WIDTH = 4

FFN_ROWS = 512
FFN_FCHUNK = 256
PROJ_ROWS = 1024
PROJ_COLS = 1664
SEQ_ROWS = 256
MERGE_ROWS = 512


def _rmsnorm(x, w):
    ms = jnp.mean(x * x, axis=-1, keepdims=True)
    return x * lax.rsqrt(ms + EPS) * w


def _const_spec(shape, single_buffer=False):
    nd = len(shape)
    kw = {"pipeline_mode": pl.Buffered(1)} if single_buffer else {}
    return pl.BlockSpec(shape, lambda *_: (0,) * nd, **kw)


def _params(*sem):
    return pltpu.CompilerParams(dimension_semantics=sem, vmem_limit_bytes=VMEM_LIMIT_BYTES)


def _ffn_body(x_ref, nw_ref, wg_ref, wu_ref, wd_ref, *rest, final):
    o_ref = rest[-1]
    x = x_ref[...]
    h = _rmsnorm(x, nw_ref[...]).astype(BF16)
    acc = jnp.zeros(x.shape, F32)
    for c in range(wg_ref.shape[1] // FFN_FCHUNK):
        cols = slice(c * FFN_FCHUNK, (c + 1) * FFN_FCHUNK)
        g = jnp.dot(h, wg_ref[:, cols], preferred_element_type=F32)
        u = jnp.dot(h, wu_ref[:, cols], preferred_element_type=F32)
        a = (g * jax.nn.sigmoid(g) * u).astype(BF16)
        acc = acc + jnp.dot(a, wd_ref[cols, :], preferred_element_type=F32)
    y = x + 0.5 * acc
    if final:
        y = _rmsnorm(y, rest[0][...])
    o_ref[...] = y


def _ffn(x, nw, wg, wu, wd, final_w=None):
    n, d = x.shape
    d_ff = wg.shape[1]
    final = final_w is not None
    in_specs = [
        pl.BlockSpec((FFN_ROWS, d), lambda i: (i, 0)),
        _const_spec((1, d)),
        _const_spec((d, d_ff), single_buffer=True),
        _const_spec((d, d_ff), single_buffer=True),
        _const_spec((d_ff, d), single_buffer=True),
    ]
    args = [x, nw, wg, wu, wd]
    if final:
        in_specs.append(_const_spec((1, d)))
        args.append(final_w)
    return pl.pallas_call(
        functools.partial(_ffn_body, final=final),
        grid=(n // FFN_ROWS,),
        in_specs=in_specs,
        out_specs=pl.BlockSpec((FFN_ROWS, d), lambda i: (i, 0)),
        out_shape=jax.ShapeDtypeStruct((n, d), F32),
        compiler_params=_params("parallel"),
        name="ffn_final" if final else "ffn",
    )(*args)


def _proj_body(x_ref, nw_ref, w_ref, o_ref, h_ref):
    @pl.when(pl.program_id(1) == 0)
    def _():
        h_ref[...] = _rmsnorm(x_ref[...], nw_ref[...]).astype(BF16)

    o_ref[...] = jnp.dot(h_ref[...], w_ref[...], preferred_element_type=F32)


def _proj(x, nw, w):
    n, d = x.shape
    cols = w.shape[1]
    return pl.pallas_call(
        _proj_body,
        grid=(n // PROJ_ROWS, cols // PROJ_COLS),
        in_specs=[
            pl.BlockSpec((PROJ_ROWS, d), lambda i, j: (i, 0)),
            _const_spec((1, d)),
            pl.BlockSpec((d, PROJ_COLS), lambda i, j: (0, j)),
        ],
        out_specs=pl.BlockSpec((PROJ_ROWS, PROJ_COLS), lambda i, j: (i, j)),
        out_shape=jax.ShapeDtypeStruct((n, cols), F32),
        scratch_shapes=[pltpu.VMEM((PROJ_ROWS, d), BF16)],
        compiler_params=_params("parallel", "arbitrary"),
        name="mixer_in_proj",
    )(x, nw, w)


def _bcast_last_row(v):
    return jnp.broadcast_to(v[SUBLANES - 1:SUBLANES, :], v.shape)


def _s5_build_tables(bmat_ref, ct_ref, pwr_ref, pwi_ref, mt_ref, g_ref, ht_ref):
    ts = S5_TILE_STATES
    bre, bim = bmat_ref[0, :, :ts], bmat_ref[0, :, ts:]
    cre, cim = ct_ref[0, :, :ts], ct_ref[0, :, ts:]
    cm = jnp.concatenate([cre, -cim], axis=1)
    zero_block = jnp.zeros((LANES, LANES), BF16)
    for tau in range(S5_T):
        pr, pi = pwr_ref[tau:tau + 1, :], pwi_ref[tau:tau + 1, :]
        bw = jnp.concatenate([bre * pr - bim * pi, bre * pi + bim * pr], axis=1)
        t_in = S5_T - 1 - tau
        g_ref[t_in * LANES:(t_in + 1) * LANES, :] = bw.astype(BF16)
        k_tau = lax.dot_general(bw, cm, (((1,), (1,)), ((), ())), preferred_element_type=F32,
                                precision=lax.Precision.HIGHEST).astype(BF16)
        for t_in in range(S5_T - tau):
            t_out = t_in + tau
            mt_ref[t_in * LANES:(t_in + 1) * LANES, t_out * LANES:(t_out + 1) * LANES] = k_tau
        pr1, pi1 = pwr_ref[tau + 1:tau + 2, :], pwi_ref[tau + 1:tau + 2, :]
        ht_ref[tau * LANES:(tau + 1) * LANES, :] = jnp.concatenate(
            [cre * pr1 - cim * pi1, -(cre * pi1 + cim * pr1)], axis=1).astype(BF16)
    steps_per_block = S5_COL_BLOCK // LANES
    for t_out in range(S5_T):
        for t_in in range(t_out + 1, (t_out // steps_per_block + 1) * steps_per_block):
            mt_ref[t_in * LANES:(t_in + 1) * LANES, t_out * LANES:(t_out + 1) * LANES] = zero_block


def _s5_body(u_ref, bmat_ref, ct_ref, pwr_ref, pwi_ref, tre_ref, tim_ref, d_ref, o_ref,
             mt_ref, g_ref, ht_ref):
    @pl.when(pl.program_id(1) == 0)
    def _():
        _s5_build_tables(bmat_ref, ct_ref, pwr_ref, pwi_ref, mt_ref, g_ref, ht_ref)

    seq = u_ref.shape[0]
    nch = seq // S5_T
    ts = S5_TILE_STATES
    pieces = [u_ref[pl.ds(t, nch, stride=S5_T), :] for t in range(S5_T)]
    ucat = jnp.concatenate(pieces, axis=1).astype(BF16)
    x_end = jnp.dot(ucat, g_ref[...], preferred_element_type=F32)

    row_id = lax.broadcasted_iota(jnp.int32, (SUBLANES, ts), 0)
    cr = jnp.zeros((SUBLANES, ts), F32)
    ci = jnp.zeros((SUBLANES, ts), F32)
    pr = tre_ref[SCAN_CARRY]
    pi = tim_ref[SCAN_CARRY]
    prev_r, prev_i = [], []
    for k in range(nch // SUBLANES):
        rs = slice(k * SUBLANES, (k + 1) * SUBLANES)
        xr = x_end[rs, :ts]
        xi = x_end[rs, ts:]
        for s in range(LOG2_SUBLANES):
            ar = tre_ref[s]
            ai = tim_ref[s]
            sr = pltpu.roll(xr, 1 << s, axis=0)
            si = pltpu.roll(xi, 1 << s, axis=0)
            xr, xi = xr + ar * sr - ai * si, xi + ar * si + ai * sr
        xr, xi = xr + pr * cr - pi * ci, xi + pr * ci + pi * cr
        prev_r.append(jnp.where(row_id == 0, cr, pltpu.roll(xr, 1, axis=0)))
        prev_i.append(jnp.where(row_id == 0, ci, pltpu.roll(xi, 1, axis=0)))
        cr = _bcast_last_row(xr)
        ci = _bcast_last_row(xi)
    xprev = jnp.concatenate(
        [jnp.concatenate(prev_r, axis=0), jnp.concatenate(prev_i, axis=0)], axis=1).astype(BF16)
    y_carry = lax.dot_general(xprev, ht_ref[...], (((1,), (1,)), ((), ())),
                              preferred_element_type=F32)

    steps_per_block = S5_COL_BLOCK // LANES
    for c in range(S5_T // steps_per_block):
        cols = slice(c * S5_COL_BLOCK, (c + 1) * S5_COL_BLOCK)
        k_hi = (c + 1) * S5_COL_BLOCK
        y_c = y_carry[:, cols] + jnp.dot(ucat[:, :k_hi], mt_ref[:k_hi, cols],
                                         preferred_element_type=F32)
        for tt in range(steps_per_block):
            t = c * steps_per_block + tt
            y_t = y_c[:, tt * LANES:(tt + 1) * LANES] + d_ref[...] * pieces[t]
            o_ref[pl.ds(t, nch, stride=S5_T), :] = jax.nn.gelu(y_t)


def _s5(proj, col_block, batch, seq, tabs, d_skip):
    n = proj.shape[0]
    n_tiles = BRANCH_WIDTH // LANES
    tl = S5_T * LANES
    ts = S5_TILE_STATES
    return pl.pallas_call(
        _s5_body,
        grid=(n_tiles, batch),
        in_specs=[
            pl.BlockSpec((seq, LANES), lambda j, b: (b, col_block * n_tiles + j)),
            pl.BlockSpec((1, LANES, 2 * ts), lambda j, b: (j, 0, 0)),
            pl.BlockSpec((1, LANES, 2 * ts), lambda j, b: (j, 0, 0)),
            pl.BlockSpec((S5_T + 1, ts), lambda j, b: (0, j)),
            pl.BlockSpec((S5_T + 1, ts), lambda j, b: (0, j)),
            pl.BlockSpec((SCAN_TABLES, SUBLANES, ts), lambda j, b: (0, 0, j)),
            pl.BlockSpec((SCAN_TABLES, SUBLANES, ts), lambda j, b: (0, 0, j)),
            pl.BlockSpec((1, LANES), lambda j, b: (0, j)),
        ],
        out_specs=pl.BlockSpec((seq, LANES), lambda j, b: (b, j)),
        out_shape=jax.ShapeDtypeStruct((n, BRANCH_WIDTH), F32),
        scratch_shapes=[
            pltpu.VMEM((tl, tl), BF16),
            pltpu.VMEM((tl, 2 * ts), BF16),
            pltpu.VMEM((tl, 2 * ts), BF16),
        ],
        compiler_params=_params("parallel", "arbitrary"),
        name="s5_mixer",
    )(proj, tabs["bmat"], tabs["ct"], tabs["pwr"], tabs["pwi"], tabs["tre"], tabs["tim"], d_skip)


def _rglru_body(x_ref, g_ref, cw_ref, cb_ref, wa_ref, ba_ref, wx_ref, bx_ref, c1_ref, o_ref,
                xbuf_ref, carry_ref):
    rows = x_ref.shape[0]
    first = pl.program_id(1) == 0

    @pl.when(first)
    def _():
        xbuf_ref[0:SUBLANES, :] = jnp.zeros((SUBLANES, x_ref.shape[1]), F32)
        carry_ref[...] = jnp.zeros(carry_ref.shape, F32)

    @pl.when(jnp.logical_not(first))
    def _():
        xbuf_ref[0:SUBLANES, :] = xbuf_ref[rows:rows + SUBLANES, :]

    xbuf_ref[SUBLANES:SUBLANES + rows, :] = x_ref[...]
    xc = cb_ref[...]
    for k in range(CONV_WIDTH):
        off = SUBLANES - (CONV_WIDTH - 1) + k
        xc = xc + cw_ref[k:k + 1, :] * xbuf_ref[off:off + rows, :]
    xcb = xc.astype(BF16)
    r = jax.nn.sigmoid(jnp.dot(xcb, wa_ref[...], preferred_element_type=F32) + ba_ref[...])
    i = jax.nn.sigmoid(jnp.dot(xcb, wx_ref[...], preferred_element_type=F32) + bx_ref[...])
    log_a = c1_ref[...] * r
    a_all = jnp.exp(log_a)
    b_all = jnp.sqrt(1.0 - a_all * a_all) * (i * xc)
    row_id = lax.broadcasted_iota(jnp.int32, (SUBLANES, x_ref.shape[1]), 0)
    carry = carry_ref[...]
    h_blocks = []
    for k in range(rows // SUBLANES):
        rs = slice(k * SUBLANES, (k + 1) * SUBLANES)
        a = a_all[rs]
        b = b_all[rs]
        for s in range(LOG2_SUBLANES):
            d = 1 << s
            m = row_id >= d
            sa = jnp.where(m, pltpu.roll(a, d, axis=0), 1.0)
            sb = jnp.where(m, pltpu.roll(b, d, axis=0), 0.0)
            a, b = a * sa, a * sb + b
        h = a * carry + b
        carry = _bcast_last_row(h)
        h_blocks.append(h)
    carry_ref[...] = carry
    o_ref[...] = jnp.concatenate(h_blocks, axis=0) * jax.nn.gelu(g_ref[...])


def _rglru(proj, x_block, gate_block, batch, seq, conv_w, conv_b, wa, ba, wx, bx, c1):
    n = proj.shape[0]
    bw = BRANCH_WIDTH
    steps = seq // SEQ_ROWS
    row_spec = lambda cb: pl.BlockSpec((SEQ_ROWS, bw), lambda b, l: (b * steps + l, cb))
    return pl.pallas_call(
        _rglru_body,
        grid=(batch, steps),
        in_specs=[
            row_spec(x_block), row_spec(gate_block),
            _const_spec((CONV_WIDTH, bw)), _const_spec((1, bw)),
            _const_spec((bw, bw)), _const_spec((1, bw)),
            _const_spec((bw, bw)), _const_spec((1, bw)),
            _const_spec((1, bw)),
        ],
        out_specs=pl.BlockSpec((SEQ_ROWS, bw), lambda b, l: (b * steps + l, 0)),
        out_shape=jax.ShapeDtypeStruct((n, bw), F32),
        scratch_shapes=[
            pltpu.VMEM((SEQ_ROWS + SUBLANES, bw), F32),
            pltpu.VMEM((SUBLANES, bw), F32),
        ],
        compiler_params=_params("parallel", "arbitrary"),
        name="rglru_mixer",
    )(proj, proj, conv_w, conv_b, wa, ba, wx, bx, c1)


def _rows_bcast(v, row):
    return jnp.broadcast_to(v[row:row + 1, :], (SUBLANES, v.shape[1]))


def _boundary_rows(b, m):
    c, w = b.shape
    if 2 * m >= SUBLANES:
        pieces = []
        for base in range(0, c, 2 * m):
            r = base + m - 1
            pieces.append(jnp.broadcast_to(b[r:r + 1, :], (2 * m, w)))
        return jnp.concatenate(pieces, axis=0)
    sub = lax.broadcasted_iota(jnp.int32, (SUBLANES, w), 0)
    pieces = []
    for base in range(0, c, SUBLANES):
        blk = b[base:base + SUBLANES, :]
        if m == 2:
            out = jnp.where(sub < 4, _rows_bcast(blk, 1), _rows_bcast(blk, 5))
        else:
            out = jnp.where(sub < 2, _rows_bcast(blk, 0),
                            jnp.where(sub < 4, _rows_bcast(blk, 2),
                                      jnp.where(sub < 6, _rows_bcast(blk, 4), _rows_bcast(blk, 6))))
        pieces.append(out)
    return jnp.concatenate(pieces, axis=0)


def _hgrn2_body(q_ref, f_ref, v_ref, g_ref, lb_ref, nw_ref, o_ref, st_ref):
    @pl.when(pl.program_id(1) == 0)
    def _():
        st_ref[...] = jnp.zeros(st_ref.shape, F32)

    rows = q_ref.shape[0]
    c = HG_CHUNK
    t_id = lax.broadcasted_iota(jnp.int32, (c, c), 0)
    s_id = lax.broadcasted_iota(jnp.int32, (c, c), 1)
    row_id = lax.broadcasted_iota(jnp.int32, (c, HG_DK), 0)
    levels = [1 << e for e in range(int(math.log2(c)))]
    same_block = {m: (t_id >> (e + 1)) == (s_id >> (e + 1)) for e, m in enumerate(levels)}
    right_half = {m: (row_id & m) != 0 for m in levels}
    tril = (s_id <= t_id).astype(F32)

    for hd in range(HG_HEADS):
        lanes = slice(hd * HG_DK, (hd + 1) * HG_DK)
        lb = lb_ref[:, lanes]
        st = st_ref[hd]
        for ck in range(rows // c):
            rs = slice(ck * c, (ck + 1) * c)
            qraw = q_ref[rs, lanes]
            z = f_ref[rs, lanes]
            v = v_ref[rs, lanes]
            q = qraw * jax.nn.sigmoid(qraw)
            logf = jnp.log(lb + (1.0 - lb) * jax.nn.sigmoid(z))
            k = (1.0 - lb) * jax.nn.sigmoid(-z)
            b = jnp.dot(tril, logf, preferred_element_type=F32, precision=lax.Precision.HIGHEST)
            vb = v.astype(BF16)
            scores = jnp.zeros((c, c), F32)
            for m in levels:
                bm = _boundary_rows(b, m)
                rh = right_half[m]
                w = jnp.exp(jnp.where(rh, b - bm, bm - b))
                qs = jnp.where(rh, q * w, 0.0).astype(BF16)
                ks = jnp.where(rh, 0.0, k * w).astype(BF16)
                part = lax.dot_general(qs, ks, (((1,), (1,)), ((), ())), preferred_element_type=F32)
                scores = scores + jnp.where(same_block[m], part, 0.0)
            o = jnp.dot(scores.astype(BF16), vb, preferred_element_type=F32)
            o = o + jnp.sum(q * k, axis=-1, keepdims=True) * v
            qd = (q * jnp.exp(b)).astype(BF16)
            o = o + lax.dot_general(qd, st.astype(BF16), (((1,), (1,)), ((), ())),
                                    preferred_element_type=F32)
            b_last = b[c - 1:c, :]
            kd = (k * jnp.exp(b_last - b)).astype(BF16)
            st = st * jnp.exp(b_last) + lax.dot_general(
                vb, kd, (((0,), (0,)), ((), ())), preferred_element_type=F32)
            o = o * lax.rsqrt(jnp.mean(o * o, axis=-1, keepdims=True) + EPS) * nw_ref[:, lanes]
            graw = g_ref[rs, lanes]
            o_ref[rs, lanes] = o * (graw * jax.nn.sigmoid(graw))
        st_ref[hd] = st


def _hgrn2(proj, q_block, f_block, v_block, g_block, batch, seq, lb, norm_w):
    n = proj.shape[0]
    bw = BRANCH_WIDTH
    steps = seq // SEQ_ROWS
    row_spec = lambda cb: pl.BlockSpec((SEQ_ROWS, bw), lambda b, l: (b * steps + l, cb))
    return pl.pallas_call(
        _hgrn2_body,
        grid=(batch, steps),
        in_specs=[row_spec(q_block), row_spec(f_block), row_spec(v_block), row_spec(g_block),
                  _const_spec((1, bw)), _const_spec((1, bw))],
        out_specs=pl.BlockSpec((SEQ_ROWS, bw), lambda b, l: (b * steps + l, 0)),
        out_shape=jax.ShapeDtypeStruct((n, bw), F32),
        scratch_shapes=[pltpu.VMEM((HG_HEADS, HG_DV, HG_DK), F32)],
        compiler_params=_params("parallel", "arbitrary"),
        name="hgrn2_mixer",
    )(proj, proj, proj, proj, lb, norm_w)


def _merge_body(x_ref, za_ref, yb_ref, yc_ref, *rest):
    gate_refs, (gw_ref, gb_ref, p_ref, wo_ref, o_ref) = rest[:-5], rest[-5:]
    halves = len(gate_refs) // N_BRANCH
    za = za_ref[...]
    glu = jnp.dot(za.astype(BF16), gw_ref[...], preferred_element_type=F32) + gb_ref[...]
    ya = za * jax.nn.sigmoid(glu)
    merged = None
    for n, y in enumerate((ya, yb_ref[...], yc_ref[...])):
        up = jnp.dot(y.astype(BF16), p_ref[n], preferred_element_type=F32)
        gate = jnp.concatenate([g[...] for g in gate_refs[n * halves:(n + 1) * halves]], axis=1)
        term = jax.nn.sigmoid(gate) * up
        merged = term if merged is None else merged + term
    o_ref[...] = x_ref[...] + jnp.dot(merged.astype(BF16), wo_ref[...], preferred_element_type=F32)


def _merge(x, za, yb, yc, proj, first_gate_block, glu_w, glu_b, p, wo):
    n, d = x.shape
    bw = BRANCH_WIDTH
    n_gate_blocks = N_BRANCH * d // bw
    y_spec = pl.BlockSpec((MERGE_ROWS, bw), lambda i: (i, 0))
    g_spec = lambda cb: pl.BlockSpec((MERGE_ROWS, bw), lambda i: (i, cb))
    return pl.pallas_call(
        _merge_body,
        grid=(n // MERGE_ROWS,),
        in_specs=[pl.BlockSpec((MERGE_ROWS, d), lambda i: (i, 0)), y_spec, y_spec, y_spec]
        + [g_spec(first_gate_block + k) for k in range(n_gate_blocks)]
        + [_const_spec((bw, bw)), _const_spec((1, bw)),
           _const_spec((N_BRANCH, bw, d)), _const_spec((d, d))],
        out_specs=pl.BlockSpec((MERGE_ROWS, d), lambda i: (i, 0)),
        out_shape=jax.ShapeDtypeStruct((n, d), F32),
        compiler_params=_params("parallel"),
        name="merge_out_proj",
    )(x, za, yb, yc, *([proj] * n_gate_blocks), glu_w, glu_b, p, wo)


def _s5_tables(lam_re, lam_im, log_dt, b_re, b_im, c_re, c_im):
    groups, states = lam_re.shape
    n_tiles = groups // S5_GROUPS_PER_TILE
    lr = jnp.minimum(lam_re.astype(F32), S5_EIG_MAX)
    li = lam_im.astype(F32)
    dt = jnp.exp(log_dt.astype(F32))[:, None]
    mag = jnp.exp(lr * dt)
    ar = mag * jnp.cos(li * dt)
    ai = mag * jnp.sin(li * dt)
    den = lr * lr + li * li
    fr = ((ar - 1.0) * lr + ai * li) / den
    fi = (ai * lr - (ar - 1.0) * li) / den
    br, bi = b_re.astype(F32), b_im.astype(F32)
    bbr = fr[..., None] * br - fi[..., None] * bi
    bbi = fr[..., None] * bi + fi[..., None] * br
    gpt = S5_GROUPS_PER_TILE
    eye = jnp.eye(gpt, dtype=F32)

    def in_tiles(m):
        m = m.reshape(n_tiles, gpt, states, S5_GROUP)
        return jnp.einsum('jgpc,gh->jgchp', m, eye).reshape(n_tiles, LANES, S5_TILE_STATES)

    def out_tiles(m):
        m = m.reshape(n_tiles, gpt, S5_GROUP, states)
        return jnp.einsum('jgcp,gh->jgchp', m, eye).reshape(n_tiles, LANES, S5_TILE_STATES)

    bmat = jnp.concatenate([in_tiles(bbr), in_tiles(bbi)], axis=2)
    ct = jnp.concatenate([out_tiles(c_re.astype(F32)), out_tiles(c_im.astype(F32))], axis=2)

    def power(kk):
        m = jnp.exp(lr * dt * kk)
        return m * jnp.cos(li * dt * kk), m * jnp.sin(li * dt * kk)

    pwr, pwi = power(jnp.arange(S5_T + 1, dtype=F32)[:, None, None])
    pwr = pwr.reshape(S5_T + 1, groups * states)
    pwi = pwi.reshape(S5_T + 1, groups * states)

    rows = jnp.arange(SUBLANES, dtype=F32)
    tre, tim = [], []
    for s in range(LOG2_SUBLANES):
        d = float(1 << s)
        sr, si = power(jnp.full((SUBLANES, 1, 1), d * S5_T, F32))
        keep = (rows >= d)[:, None, None]
        tre.append(jnp.where(keep, sr, 0.0))
        tim.append(jnp.where(keep, si, 0.0))
    sr, si = power((rows[:, None, None] + 1.0) * S5_T)
    tre.append(sr)
    tim.append(si)
    tre = jnp.stack(tre).reshape(SCAN_TABLES, SUBLANES, groups * states)
    tim = jnp.stack(tim).reshape(SCAN_TABLES, SUBLANES, groups * states)
    return {"bmat": bmat, "ct": ct, "pwr": pwr, "pwi": pwi, "tre": tre, "tim": tim}


def _block_diag(w):
    nb, bi, bo = w.shape
    eye = jnp.eye(nb, dtype=w.dtype)
    return jnp.einsum('hij,hk->hikj', w, eye).reshape(nb * bi, nb * bo)


def kernel(x, norm_w, final_norm_w, ffn_gate, ffn_up, ffn_down, w_in, branch_proj, w_out, s5_lambda_re, s5_lambda_im, s5_log_dt, s5_b_re, s5_b_im, s5_c_re, s5_c_im, s5_d, s5_glu_w, s5_glu_b, hg_lb_logits, hg_norm_w, rg_conv_w, rg_conv_b, rg_wa, rg_ba, rg_wx, rg_bx, rg_lambda):
    batch, seq, d = x.shape
    depth = norm_w.shape[0]
    col = {name: k for k, name in enumerate(("u", "q", "f", "v", "g", "xc", "gc", "merge"))}

    p = jax.nn.softmax(hg_lb_logits.astype(F32), axis=0)
    lower_bounds = jnp.cumsum(p, axis=0) - p[0]

    h = x.reshape(batch * seq, d)
    row = lambda v: v.reshape(1, -1).astype(F32)
    for l in range(depth):
        def ffn_w(j):
            return (ffn_gate[l, j].astype(BF16), ffn_up[l, j].astype(BF16),
                    ffn_down[l, j].astype(BF16))

        h = _ffn(h, row(norm_w[l, 0]), *ffn_w(0))
        proj = _proj(h, row(norm_w[l, 1]), w_in[l].astype(BF16))
        tabs = _s5_tables(s5_lambda_re[l], s5_lambda_im[l], s5_log_dt[l], s5_b_re[l], s5_b_im[l],
                          s5_c_re[l], s5_c_im[l])
        za = _s5(proj, col["u"], batch, seq, tabs, row(s5_d[l]))
        yb = _hgrn2(proj, col["q"], col["f"], col["v"], col["g"], batch, seq,
                    row(lower_bounds[l]), row(hg_norm_w[l]))
        c1 = -RG_C * jax.nn.softplus(-rg_lambda[l].astype(F32))
        yc = _rglru(proj, col["xc"], col["gc"], batch, seq, rg_conv_w[l].astype(F32),
                    row(rg_conv_b[l]), _block_diag(rg_wa[l]).astype(BF16), row(rg_ba[l]),
                    _block_diag(rg_wx[l]).astype(BF16), row(rg_bx[l]), row(c1))
        h = _merge(h, za, yb, yc, proj, col["merge"], s5_glu_w[l].astype(BF16),
                   row(s5_glu_b[l]), branch_proj[l].astype(BF16),
                   w_out[l].astype(BF16))
        last = l == depth - 1
        h = _ffn(h, row(norm_w[l, 2]), *ffn_w(1), final_w=row(final_norm_w) if last else None)
    return h.reshape(batch, seq, d)
```

```python
import functools
import math

import jax
import jax.numpy as jnp
from jax import lax
from jax.experimental import pallas as pl
from jax.experimental.pallas import tpu as pltpu

F32 = jnp.float32
BF16 = jnp.bfloat16

LANES = 128
SUBLANES = 8
LOG2_SUBLANES = 3
MXU_WIDTH = 256
VMEM_LIMIT_BYTES = 56 * 1024 * 1024

SCAN_CARRY = LOG2_SUBLANES
SCAN_TABLES = LOG2_SUBLANES + 1

EPS = 1e-6
BRANCH_WIDTH = 512
N_BRANCH = 3
S5_GROUP = 16
S5_STATE = 64
S5_EIG_MAX = -1e-4
S5_GROUPS_PER_TILE = LANES // S5_GROUP
S5_TILE_STATES = S5_GROUPS_PER_TILE * S5_STATE
S5_T = 16
S5_COL_BLOCK = MXU_WIDTH
HG_HEADS = 4
HG_DK = 128
HG_DV = 128
HG_CHUNK = 64
RG_BLOCKS = 8
RG_BLOCK = 64
RG_C = 8.0
CONV_WIDTH = 4

FFN_ROWS = 512
FFN_FCHUNK = 256
PROJ_ROWS = 512
SEQ_ROWS = 256
MERGE_ROWS = 512


def _rmsnorm(x, w):
    ms = jnp.mean(x * x, axis=-1, keepdims=True)
    return x * lax.rsqrt(ms + EPS) * w


def _const_spec(shape, single_buffer=False):
    nd = len(shape)
    kw = {"pipeline_mode": pl.Buffered(1)} if single_buffer else {}
    return pl.BlockSpec(shape, lambda *_: (0,) * nd, **kw)


def _params(*sem):
    return pltpu.CompilerParams(dimension_semantics=sem, vmem_limit_bytes=VMEM_LIMIT_BYTES)


def _ffn_body(x_ref, nw_ref, wg_ref, wu_ref, wd_ref, *rest, final):
    o_ref = rest[-1]
    x = x_ref[...]
    h = _rmsnorm(x, nw_ref[...]).astype(BF16)
    acc = jnp.zeros(x.shape, F32)
    for c in range(wg_ref.shape[1] // FFN_FCHUNK):
        cols = slice(c * FFN_FCHUNK, (c + 1) * FFN_FCHUNK)
        g = jnp.dot(h, wg_ref[:, cols], preferred_element_type=F32)
        u = jnp.dot(h, wu_ref[:, cols], preferred_element_type=F32)
        a = (g * jax.nn.sigmoid(g) * u).astype(BF16)
        acc = acc + jnp.dot(a, wd_ref[cols, :], preferred_element_type=F32)
    y = x + 0.5 * acc
    if final:
        y = _rmsnorm(y, rest[0][...])
    o_ref[...] = y


def _ffn(x, nw, wg, wu, wd, final_w=None):
    n, d = x.shape
    d_ff = wg.shape[1]
    final = final_w is not None
    in_specs = [
        pl.BlockSpec((FFN_ROWS, d), lambda i: (i, 0)),
        _const_spec((1, d)),
        _const_spec((d, d_ff), single_buffer=True),
        _const_spec((d, d_ff), single_buffer=True),
        _const_spec((d_ff, d), single_buffer=True),
    ]
    args = [x, nw, wg, wu, wd]
    if final:
        in_specs.append(_const_spec((1, d)))
        args.append(final_w)
    return pl.pallas_call(
        functools.partial(_ffn_body, final=final),
        grid=(n // FFN_ROWS,),
        in_specs=in_specs,
        out_specs=pl.BlockSpec((FFN_ROWS, d), lambda i: (i, 0)),
        out_shape=jax.ShapeDtypeStruct((n, d), F32),
        compiler_params=_params("parallel"),
        name="ffn_final" if final else "ffn",
    )(*args)


def _proj_body(x_ref, nw_ref, w_ref, o_ref):
    h = _rmsnorm(x_ref[...], nw_ref[...]).astype(BF16)
    o_ref[...] = jnp.dot(h, w_ref[...], preferred_element_type=F32)


def _proj(x, nw, w):
    n, d = x.shape
    cols = w.shape[1]
    return pl.pallas_call(
        _proj_body,
        grid=(n // PROJ_ROWS,),
        in_specs=[
            pl.BlockSpec((PROJ_ROWS, d), lambda i: (i, 0)),
            _const_spec((1, d)),
            _const_spec((d, cols), single_buffer=True),
        ],
        out_specs=pl.BlockSpec((PROJ_ROWS, cols), lambda i: (i, 0)),
        out_shape=jax.ShapeDtypeStruct((n, cols), F32),
        compiler_params=_params("parallel"),
        name="mixer_in_proj",
    )(x, nw, w)


def _bcast_last_row(v):
    return jnp.broadcast_to(v[SUBLANES - 1:SUBLANES, :], v.shape)


def _s5_build_tables(bmat_ref, ct_ref, pwr_ref, pwi_ref, mt_ref, g_ref, ht_ref):
    ts = S5_TILE_STATES
    bre, bim = bmat_ref[0, :, :ts], bmat_ref[0, :, ts:]
    cre, cim = ct_ref[0, :, :ts], ct_ref[0, :, ts:]
    cm = jnp.concatenate([cre, -cim], axis=1)
    zero_block = jnp.zeros((LANES, LANES), BF16)
    for tau in range(S5_T):
        pr, pi = pwr_ref[tau:tau + 1, :], pwi_ref[tau:tau + 1, :]
        bw = jnp.concatenate([bre * pr - bim * pi, bre * pi + bim * pr], axis=1)
        t_in = S5_T - 1 - tau
        g_ref[t_in * LANES:(t_in + 1) * LANES, :] = bw.astype(BF16)
        k_tau = lax.dot_general(bw, cm, (((1,), (1,)), ((), ())), preferred_element_type=F32,
                                precision=lax.Precision.HIGHEST).astype(BF16)
        for t_in in range(S5_T - tau):
            t_out = t_in + tau
            mt_ref[t_in * LANES:(t_in + 1) * LANES, t_out * LANES:(t_out + 1) * LANES] = k_tau
        pr1, pi1 = pwr_ref[tau + 1:tau + 2, :], pwi_ref[tau + 1:tau + 2, :]
        ht_ref[tau * LANES:(tau + 1) * LANES, :] = jnp.concatenate(
            [cre * pr1 - cim * pi1, -(cre * pi1 + cim * pr1)], axis=1).astype(BF16)
    steps_per_block = S5_COL_BLOCK // LANES
    for t_out in range(S5_T):
        for t_in in range(t_out + 1, (t_out // steps_per_block + 1) * steps_per_block):
            mt_ref[t_in * LANES:(t_in + 1) * LANES, t_out * LANES:(t_out + 1) * LANES] = zero_block


def _s5_body(u_ref, bmat_ref, ct_ref, pwr_ref, pwi_ref, tre_ref, tim_ref, d_ref, o_ref,
             mt_ref, g_ref, ht_ref):
    @pl.when(pl.program_id(1) == 0)
    def _():
        _s5_build_tables(bmat_ref, ct_ref, pwr_ref, pwi_ref, mt_ref, g_ref, ht_ref)

    seq = u_ref.shape[0]
    nch = seq // S5_T
    ts = S5_TILE_STATES
    pieces = [u_ref[pl.ds(t, nch, stride=S5_T), :] for t in range(S5_T)]
    ucat = jnp.concatenate(pieces, axis=1).astype(BF16)
    x_end = jnp.dot(ucat, g_ref[...], preferred_element_type=F32)

    row_id = lax.broadcasted_iota(jnp.int32, (SUBLANES, ts), 0)
    cr = jnp.zeros((SUBLANES, ts), F32)
    ci = jnp.zeros((SUBLANES, ts), F32)
    pr = tre_ref[SCAN_CARRY]
    pi = tim_ref[SCAN_CARRY]
    prev_r, prev_i = [], []
    for k in range(nch // SUBLANES):
        rs = slice(k * SUBLANES, (k + 1) * SUBLANES)
        xr = x_end[rs, :ts]
        xi = x_end[rs, ts:]
        for s in range(LOG2_SUBLANES):
            ar = tre_ref[s]
            ai = tim_ref[s]
            sr = pltpu.roll(xr, 1 << s, axis=0)
            si = pltpu.roll(xi, 1 << s, axis=0)
            xr, xi = xr + ar * sr - ai * si, xi + ar * si + ai * sr
        xr, xi = xr + pr * cr - pi * ci, xi + pr * ci + pi * cr
        prev_r.append(jnp.where(row_id == 0, cr, pltpu.roll(xr, 1, axis=0)))
        prev_i.append(jnp.where(row_id == 0, ci, pltpu.roll(xi, 1, axis=0)))
        cr = _bcast_last_row(xr)
        ci = _bcast_last_row(xi)
    xprev = jnp.concatenate(
        [jnp.concatenate(prev_r, axis=0), jnp.concatenate(prev_i, axis=0)], axis=1).astype(BF16)
    y_carry = lax.dot_general(xprev, ht_ref[...], (((1,), (1,)), ((), ())),
                              preferred_element_type=F32)

    steps_per_block = S5_COL_BLOCK // LANES
    for c in range(S5_T // steps_per_block):
        cols = slice(c * S5_COL_BLOCK, (c + 1) * S5_COL_BLOCK)
        k_hi = (c + 1) * S5_COL_BLOCK
        y_c = y_carry[:, cols] + jnp.dot(ucat[:, :k_hi], mt_ref[:k_hi, cols],
                                         preferred_element_type=F32)
        for tt in range(steps_per_block):
            t = c * steps_per_block + tt
            y_t = y_c[:, tt * LANES:(tt + 1) * LANES] + d_ref[...] * pieces[t]
            o_ref[pl.ds(t, nch, stride=S5_T), :] = jax.nn.gelu(y_t)


def _s5(proj, col_block, batch, seq, tabs, d_skip):
    n = proj.shape[0]
    n_tiles = BRANCH_WIDTH // LANES
    tl = S5_T * LANES
    ts = S5_TILE_STATES
    return pl.pallas_call(
        _s5_body,
        grid=(n_tiles, batch),
        in_specs=[
            pl.BlockSpec((seq, LANES), lambda j, b: (b, col_block * n_tiles + j)),
            pl.BlockSpec((1, LANES, 2 * ts), lambda j, b: (j, 0, 0)),
            pl.BlockSpec((1, LANES, 2 * ts), lambda j, b: (j, 0, 0)),
            pl.BlockSpec((S5_T + 1, ts), lambda j, b: (0, j)),
            pl.BlockSpec((S5_T + 1, ts), lambda j, b: (0, j)),
            pl.BlockSpec((SCAN_TABLES, SUBLANES, ts), lambda j, b: (0, 0, j)),
            pl.BlockSpec((SCAN_TABLES, SUBLANES, ts), lambda j, b: (0, 0, j)),
            pl.BlockSpec((1, LANES), lambda j, b: (0, j)),
        ],
        out_specs=pl.BlockSpec((seq, LANES), lambda j, b: (b, j)),
        out_shape=jax.ShapeDtypeStruct((n, BRANCH_WIDTH), F32),
        scratch_shapes=[
            pltpu.VMEM((tl, tl), BF16),
            pltpu.VMEM((tl, 2 * ts), BF16),
            pltpu.VMEM((tl, 2 * ts), BF16),
        ],
        compiler_params=_params("parallel", "arbitrary"),
        name="s5_mixer",
    )(proj, tabs["bmat"], tabs["ct"], tabs["pwr"], tabs["pwi"], tabs["tre"], tabs["tim"], d_skip)


def _rglru_body(x_ref, g_ref, cw_ref, cb_ref, wa_ref, ba_ref, wx_ref, bx_ref, c1_ref, o_ref,
                xbuf_ref, carry_ref):
    rows = x_ref.shape[0]
    first = pl.program_id(1) == 0

    @pl.when(first)
    def _():
        xbuf_ref[0:SUBLANES, :] = jnp.zeros((SUBLANES, x_ref.shape[1]), F32)
        carry_ref[...] = jnp.zeros(carry_ref.shape, F32)

    @pl.when(jnp.logical_not(first))
    def _():
        xbuf_ref[0:SUBLANES, :] = xbuf_ref[rows:rows + SUBLANES, :]

    xbuf_ref[SUBLANES:SUBLANES + rows, :] = x_ref[...]
    xc = cb_ref[...]
    for k in range(CONV_WIDTH):
        off = SUBLANES - (CONV_WIDTH - 1) + k
        xc = xc + cw_ref[k:k + 1, :] * xbuf_ref[off:off + rows, :]
    xcb = xc.astype(BF16)
    r = jax.nn.sigmoid(jnp.dot(xcb, wa_ref[...], preferred_element_type=F32) + ba_ref[...])
    i = jax.nn.sigmoid(jnp.dot(xcb, wx_ref[...], preferred_element_type=F32) + bx_ref[...])
    log_a = c1_ref[...] * r
    a_all = jnp.exp(log_a)
    b_all = jnp.sqrt(1.0 - a_all * a_all) * (i * xc)
    row_id = lax.broadcasted_iota(jnp.int32, (SUBLANES, x_ref.shape[1]), 0)
    carry = carry_ref[...]
    h_blocks = []
    for k in range(rows // SUBLANES):
        rs = slice(k * SUBLANES, (k + 1) * SUBLANES)
        a = a_all[rs]
        b = b_all[rs]
        for s in range(LOG2_SUBLANES):
            d = 1 << s
            m = row_id >= d
            sa = jnp.where(m, pltpu.roll(a, d, axis=0), 1.0)
            sb = jnp.where(m, pltpu.roll(b, d, axis=0), 0.0)
            a, b = a * sa, a * sb + b
        h = a * carry + b
        carry = _bcast_last_row(h)
        h_blocks.append(h)
    carry_ref[...] = carry
    o_ref[...] = jnp.concatenate(h_blocks, axis=0) * jax.nn.gelu(g_ref[...])


def _rglru(proj, x_block, gate_block, batch, seq, conv_w, conv_b, wa, ba, wx, bx, c1):
    n = proj.shape[0]
    bw = BRANCH_WIDTH
    steps = seq // SEQ_ROWS
    row_spec = lambda cb: pl.BlockSpec((SEQ_ROWS, bw), lambda b, l: (b * steps + l, cb))
    return pl.pallas_call(
        _rglru_body,
        grid=(batch, steps),
        in_specs=[
            row_spec(x_block), row_spec(gate_block),
            _const_spec((CONV_WIDTH, bw)), _const_spec((1, bw)),
            _const_spec((bw, bw)), _const_spec((1, bw)),
            _const_spec((bw, bw)), _const_spec((1, bw)),
            _const_spec((1, bw)),
        ],
        out_specs=pl.BlockSpec((SEQ_ROWS, bw), lambda b, l: (b * steps + l, 0)),
        out_shape=jax.ShapeDtypeStruct((n, bw), F32),
        scratch_shapes=[
            pltpu.VMEM((SEQ_ROWS + SUBLANES, bw), F32),
            pltpu.VMEM((SUBLANES, bw), F32),
        ],
        compiler_params=_params("parallel", "arbitrary"),
        name="rglru_mixer",
    )(proj, proj, conv_w, conv_b, wa, ba, wx, bx, c1)


def _rows_bcast(v, row):
    return jnp.broadcast_to(v[row:row + 1, :], (SUBLANES, v.shape[1]))


def _boundary_rows(b, m):
    c, w = b.shape
    if 2 * m >= SUBLANES:
        pieces = []
        for base in range(0, c, 2 * m):
            r = base + m - 1
            pieces.append(jnp.broadcast_to(b[r:r + 1, :], (2 * m, w)))
        return jnp.concatenate(pieces, axis=0)
    sub = lax.broadcasted_iota(jnp.int32, (SUBLANES, w), 0)
    pieces = []
    for base in range(0, c, SUBLANES):
        blk = b[base:base + SUBLANES, :]
        if m == 2:
            out = jnp.where(sub < 4, _rows_bcast(blk, 1), _rows_bcast(blk, 5))
        else:
            out = jnp.where(sub < 2, _rows_bcast(blk, 0),
                            jnp.where(sub < 4, _rows_bcast(blk, 2),
                                      jnp.where(sub < 6, _rows_bcast(blk, 4), _rows_bcast(blk, 6))))
        pieces.append(out)
    return jnp.concatenate(pieces, axis=0)


def _chunk_cumsum(x):
    sub = lax.broadcasted_iota(jnp.int32, (SUBLANES, x.shape[1]), 0)
    blocks = []
    carry = None
    for r in range(0, x.shape[0], SUBLANES):
        v = x[r:r + SUBLANES, :]
        for s in range(LOG2_SUBLANES):
            d = 1 << s
            v = v + jnp.where(sub >= d, pltpu.roll(v, d, axis=0), 0.0)
        if carry is not None:
            v = v + carry
        carry = _bcast_last_row(v)
        blocks.append(v)
    return jnp.concatenate(blocks, axis=0)


def _hgrn2_body(q_ref, f_ref, v_ref, g_ref, lb_ref, nw_ref, o_ref, st_ref):
    @pl.when(pl.program_id(1) == 0)
    def _():
        st_ref[...] = jnp.zeros(st_ref.shape, F32)

    rows = q_ref.shape[0]
    c = HG_CHUNK
    t_id = lax.broadcasted_iota(jnp.int32, (c, c), 0)
    s_id = lax.broadcasted_iota(jnp.int32, (c, c), 1)
    row_id = lax.broadcasted_iota(jnp.int32, (c, HG_DK), 0)
    levels = [1 << e for e in range(int(math.log2(c)))]
    pair_mask = {m: ((t_id >> (e + 1)) == (s_id >> (e + 1))) & ((t_id & m) != 0) & ((s_id & m) == 0)
                 for e, m in enumerate(levels)}
    right_half = {m: (row_id & m) != 0 for m in levels}

    for hd in range(HG_HEADS):
        lanes = slice(hd * HG_DK, (hd + 1) * HG_DK)
        lb = lb_ref[:, lanes]
        st = st_ref[hd]
        for ck in range(rows // c):
            rs = slice(ck * c, (ck + 1) * c)
            qraw = q_ref[rs, lanes]
            z = f_ref[rs, lanes]
            v = v_ref[rs, lanes]
            q = qraw * jax.nn.sigmoid(qraw)
            sig = jax.nn.sigmoid(z)
            f = lb + (1.0 - lb) * sig
            logf = jnp.log(f)
            k = (1.0 - lb) * (1.0 - sig)
            b = _chunk_cumsum(logf)
            vb = v.astype(BF16)
            scores = jnp.zeros((c, c), F32)
            for m in levels:
                rh = right_half[m]
                if m == 1:
                    w = jnp.where(rh, f, 1.0)
                else:
                    d = b - _boundary_rows(b, m)
                    w = jnp.exp(jnp.minimum(d, -d))
                xm = (jnp.where(rh, q, k) * w).astype(BF16)
                part = lax.dot_general(xm, xm, (((1,), (1,)), ((), ())), preferred_element_type=F32)
                scores = jnp.where(pair_mask[m], part, scores)
            o = jnp.dot(scores.astype(BF16), vb, preferred_element_type=F32)
            o = o + jnp.sum(q * k, axis=-1, keepdims=True) * v
            qd = (q * jnp.exp(b)).astype(BF16)
            o = o + lax.dot_general(qd, st.astype(BF16), (((1,), (1,)), ((), ())),
                                    preferred_element_type=F32)
            b_last = b[c - 1:c, :]
            kd = (k * jnp.exp(b_last - b)).astype(BF16)
            st = st * jnp.exp(b_last) + lax.dot_general(
                vb, kd, (((0,), (0,)), ((), ())), preferred_element_type=F32)
            o = o * lax.rsqrt(jnp.mean(o * o, axis=-1, keepdims=True) + EPS) * nw_ref[:, lanes]
            graw = g_ref[rs, lanes]
            o_ref[rs, lanes] = o * (graw * jax.nn.sigmoid(graw))
        st_ref[hd] = st


def _hgrn2(proj, q_block, f_block, v_block, g_block, batch, seq, lb, norm_w):
    n = proj.shape[0]
    bw = BRANCH_WIDTH
    steps = seq // SEQ_ROWS
    row_spec = lambda cb: pl.BlockSpec((SEQ_ROWS, bw), lambda b, l: (b * steps + l, cb))
    return pl.pallas_call(
        _hgrn2_body,
        grid=(batch, steps),
        in_specs=[row_spec(q_block), row_spec(f_block), row_spec(v_block), row_spec(g_block),
                  _const_spec((1, bw)), _const_spec((1, bw))],
        out_specs=pl.BlockSpec((SEQ_ROWS, bw), lambda b, l: (b * steps + l, 0)),
        out_shape=jax.ShapeDtypeStruct((n, bw), F32),
        scratch_shapes=[pltpu.VMEM((HG_HEADS, HG_DV, HG_DK), F32)],
        compiler_params=_params("parallel", "arbitrary"),
        name="hgrn2_mixer",
    )(proj, proj, proj, proj, lb, norm_w)


def _merge_body(x_ref, nw_ref, za_ref, yb_ref, yc_ref, wg_ref, gw_ref, gb_ref, p_ref, wo_ref, o_ref):
    x = x_ref[...]
    d = x.shape[1]
    h = _rmsnorm(x, nw_ref[...]).astype(BF16)
    za = za_ref[...]
    glu = jnp.dot(za.astype(BF16), gw_ref[...], preferred_element_type=F32) + gb_ref[...]
    ya = za * jax.nn.sigmoid(glu)
    merged = None
    for n, y in enumerate((ya, yb_ref[...], yc_ref[...])):
        up = jnp.dot(y.astype(BF16), p_ref[n], preferred_element_type=F32)
        gate = jnp.dot(h, wg_ref[:, n * d:(n + 1) * d], preferred_element_type=F32)
        term = jax.nn.sigmoid(gate) * up
        merged = term if merged is None else merged + term
    o_ref[...] = x + jnp.dot(merged.astype(BF16), wo_ref[...], preferred_element_type=F32)


def _merge(x, nw, za, yb, yc, w_gate, glu_w, glu_b, p, wo):
    n, d = x.shape
    bw = BRANCH_WIDTH
    y_spec = pl.BlockSpec((MERGE_ROWS, bw), lambda i: (i, 0))
    return pl.pallas_call(
        _merge_body,
        grid=(n // MERGE_ROWS,),
        in_specs=[pl.BlockSpec((MERGE_ROWS, d), lambda i: (i, 0)), _const_spec((1, d)),
                  y_spec, y_spec, y_spec,
                  _const_spec((d, N_BRANCH * d), single_buffer=True),
                  _const_spec((bw, bw), single_buffer=True), _const_spec((1, bw)),
                  _const_spec((N_BRANCH, bw, d), single_buffer=True),
                  _const_spec((d, d), single_buffer=True)],
        out_specs=pl.BlockSpec((MERGE_ROWS, d), lambda i: (i, 0)),
        out_shape=jax.ShapeDtypeStruct((n, d), F32),
        compiler_params=_params("parallel"),
        name="merge_out_proj",
    )(x, nw, za, yb, yc, w_gate, glu_w, glu_b, p, wo)


def _s5_tables(lam_re, lam_im, log_dt, b_re, b_im, c_re, c_im):
    groups, states = lam_re.shape
    n_tiles = groups // S5_GROUPS_PER_TILE
    lr = jnp.minimum(lam_re.astype(F32), S5_EIG_MAX)
    li = lam_im.astype(F32)
    dt = jnp.exp(log_dt.astype(F32))[:, None]
    mag = jnp.exp(lr * dt)
    ar = mag * jnp.cos(li * dt)
    ai = mag * jnp.sin(li * dt)
    den = lr * lr + li * li
    fr = ((ar - 1.0) * lr + ai * li) / den
    fi = (ai * lr - (ar - 1.0) * li) / den
    br, bi = b_re.astype(F32), b_im.astype(F32)
    bbr = fr[..., None] * br - fi[..., None] * bi
    bbi = fr[..., None] * bi + fi[..., None] * br
    gpt = S5_GROUPS_PER_TILE
    eye = jnp.eye(gpt, dtype=F32)

    def in_tiles(m):
        m = m.reshape(n_tiles, gpt, states, S5_GROUP)
        return jnp.einsum('jgpc,gh->jgchp', m, eye).reshape(n_tiles, LANES, S5_TILE_STATES)

    def out_tiles(m):
        m = m.reshape(n_tiles, gpt, S5_GROUP, states)
        return jnp.einsum('jgcp,gh->jgchp', m, eye).reshape(n_tiles, LANES, S5_TILE_STATES)

    bmat = jnp.concatenate([in_tiles(bbr), in_tiles(bbi)], axis=2)
    ct = jnp.concatenate([out_tiles(c_re.astype(F32)), out_tiles(c_im.astype(F32))], axis=2)

    def power(kk):
        m = jnp.exp(lr * dt * kk)
        return m * jnp.cos(li * dt * kk), m * jnp.sin(li * dt * kk)

    pwr, pwi = power(jnp.arange(S5_T + 1, dtype=F32)[:, None, None])
    pwr = pwr.reshape(S5_T + 1, groups * states)
    pwi = pwi.reshape(S5_T + 1, groups * states)

    rows = jnp.arange(SUBLANES, dtype=F32)
    tre, tim = [], []
    for s in range(LOG2_SUBLANES):
        d = float(1 << s)
        sr, si = power(jnp.full((SUBLANES, 1, 1), d * S5_T, F32))
        keep = (rows >= d)[:, None, None]
        tre.append(jnp.where(keep, sr, 0.0))
        tim.append(jnp.where(keep, si, 0.0))
    sr, si = power((rows[:, None, None] + 1.0) * S5_T)
    tre.append(sr)
    tim.append(si)
    tre = jnp.stack(tre).reshape(SCAN_TABLES, SUBLANES, groups * states)
    tim = jnp.stack(tim).reshape(SCAN_TABLES, SUBLANES, groups * states)
    return {"bmat": bmat, "ct": ct, "pwr": pwr, "pwi": pwi, "tre": tre, "tim": tim}


def _block_diag(w):
    nb, bi, bo = w.shape
    eye = jnp.eye(nb, dtype=w.dtype)
    return jnp.einsum('hij,hk->hikj', w, eye).reshape(nb * bi, nb * bo)


def kernel(x, norm_w, final_norm_w, ffn_gate, ffn_up, ffn_down, w_in, branch_proj, w_out, s5_lambda_re, s5_lambda_im, s5_log_dt, s5_b_re, s5_b_im, s5_c_re, s5_c_im, s5_d, s5_glu_w, s5_glu_b, hg_lb_logits, hg_norm_w, rg_conv_w, rg_conv_b, rg_wa, rg_ba, rg_wx, rg_bx, rg_lambda):
    batch, seq, d = x.shape
    depth = norm_w.shape[0]
    col = {name: k for k, name in enumerate(("u", "q", "f", "v", "g", "xc", "gc"))}
    mix_cols = len(col) * BRANCH_WIDTH

    p = jax.nn.softmax(hg_lb_logits.astype(F32), axis=0)
    lower_bounds = jnp.cumsum(p, axis=0) - p[0]

    h = x.reshape(batch * seq, d)
    row = lambda v: v.reshape(1, -1).astype(F32)
    for l in range(depth):
        def ffn_w(j):
            return (ffn_gate[l, j].astype(BF16), ffn_up[l, j].astype(BF16),
                    ffn_down[l, j].astype(BF16))

        h = _ffn(h, row(norm_w[l, 0]), *ffn_w(0))
        proj = _proj(h, row(norm_w[l, 1]), w_in[l, :, :mix_cols].astype(BF16))
        tabs = _s5_tables(s5_lambda_re[l], s5_lambda_im[l], s5_log_dt[l], s5_b_re[l], s5_b_im[l],
                          s5_c_re[l], s5_c_im[l])
        za = _s5(proj, col["u"], batch, seq, tabs, row(s5_d[l]))
        yb = _hgrn2(proj, col["q"], col["f"], col["v"], col["g"], batch, seq,
                    row(lower_bounds[l]), row(hg_norm_w[l]))
        c1 = -RG_C * jax.nn.softplus(-rg_lambda[l].astype(F32))
        yc = _rglru(proj, col["xc"], col["gc"], batch, seq, rg_conv_w[l].astype(F32),
                    row(rg_conv_b[l]), _block_diag(rg_wa[l]).astype(BF16), row(rg_ba[l]),
                    _block_diag(rg_wx[l]).astype(BF16), row(rg_bx[l]), row(c1))
        h = _merge(h, row(norm_w[l, 1]), za, yb, yc, w_in[l, :, mix_cols:].astype(BF16),
                   s5_glu_w[l].astype(BF16), row(s5_glu_b[l]), branch_proj[l].astype(BF16),
                   w_out[l].astype(BF16))
        last = l == depth - 1
        h = _ffn(h, row(norm_w[l, 2]), *ffn_w(1), final_w=row(final_norm_w) if last else None)
    return h.reshape(batch, seq, d)
```

```python
import functools
import math

import jax
import jax.numpy as jnp
from jax import lax
from jax.experimental import pallas as pl
from jax.experimental.pallas import tpu as pltpu

F32 = jnp.float32
BF16 = jnp.bfloat16

LANES = 128
SUBLANES = 8
LOG2_SUBLANES = 3
MXU_WIDTH = 256
VMEM_LIMIT_BYTES = 56 * 1024 * 1024

SCAN_CARRY = LOG2_SUBLANES
SCAN_TABLES = LOG2_SUBLANES + 1

EPS = 1e-6
BRANCH_WIDTH = 512
N_BRANCH = 3
S5_GROUP = 16
S5_STATE = 64
S5_EIG_MAX = -1e-4
S5_GROUPS_PER_TILE = LANES // S5_GROUP
S5_TILE_STATES = S5_GROUPS_PER_TILE * S5_STATE
S5_T = 8
S5_COL_BLOCK = MXU_WIDTH
HG_HEADS = 4
HG_DK = 128
HG_DV = 128
HG_CHUNK = 128
RG_BLOCKS = 8
RG_BLOCK = 64
RG_C = 8.0
CONV_WIDTH = 4

FFN_ROWS = 512
FFN_FCHUNK = 256
PROJ_ROWS = 512
SEQ_ROWS = 256
HG_ROWS = 512
MERGE_ROWS = 512


def _rmsnorm(x, w):
    ms = jnp.mean(x * x, axis=-1, keepdims=True)
    return x * lax.rsqrt(ms + EPS) * w


def _const_spec(shape, single_buffer=False):
    nd = len(shape)
    kw = {"pipeline_mode": pl.Buffered(1)} if single_buffer else {}
    return pl.BlockSpec(shape, lambda *_: (0,) * nd, **kw)


def _params(*sem):
    return pltpu.CompilerParams(dimension_semantics=sem, vmem_limit_bytes=VMEM_LIMIT_BYTES)


def _ffn_body(x_ref, nw_ref, wg_ref, wu_ref, wd_ref, *rest, final):
    o_ref = rest[-1]
    x = x_ref[...]
    h = _rmsnorm(x, nw_ref[...]).astype(BF16)
    acc = jnp.zeros(x.shape, F32)
    for c in range(wg_ref.shape[1] // FFN_FCHUNK):
        cols = slice(c * FFN_FCHUNK, (c + 1) * FFN_FCHUNK)
        g = jnp.dot(h, wg_ref[:, cols], preferred_element_type=F32)
        u = jnp.dot(h, wu_ref[:, cols], preferred_element_type=F32)
        a = (g * jax.nn.sigmoid(g) * u).astype(BF16)
        acc = acc + jnp.dot(a, wd_ref[cols, :], preferred_element_type=F32)
    y = x + 0.5 * acc
    if final:
        y = _rmsnorm(y, rest[0][...])
    o_ref[...] = y


def _ffn(x, nw, wg, wu, wd, final_w=None):
    n, d = x.shape
    d_ff = wg.shape[1]
    final = final_w is not None
    in_specs = [
        pl.BlockSpec((FFN_ROWS, d), lambda i: (i, 0)),
        _const_spec((1, d)),
        _const_spec((d, d_ff), single_buffer=True),
        _const_spec((d, d_ff), single_buffer=True),
        _const_spec((d_ff, d), single_buffer=True),
    ]
    args = [x, nw, wg, wu, wd]
    if final:
        in_specs.append(_const_spec((1, d)))
        args.append(final_w)
    return pl.pallas_call(
        functools.partial(_ffn_body, final=final),
        grid=(n // FFN_ROWS,),
        in_specs=in_specs,
        out_specs=pl.BlockSpec((FFN_ROWS, d), lambda i: (i, 0)),
        out_shape=jax.ShapeDtypeStruct((n, d), F32),
        compiler_params=_params("parallel"),
        name="ffn_final" if final else "ffn",
    )(*args)


def _proj_body(x_ref, nw_ref, w_ref, o_ref):
    h = _rmsnorm(x_ref[...], nw_ref[...]).astype(BF16)
    o_ref[...] = jnp.dot(h, w_ref[...], preferred_element_type=F32)


def _proj(x, nw, w):
    n, d = x.shape
    cols = w.shape[1]
    return pl.pallas_call(
        _proj_body,
        grid=(n // PROJ_ROWS,),
        in_specs=[
            pl.BlockSpec((PROJ_ROWS, d), lambda i: (i, 0)),
            _const_spec((1, d)),
            _const_spec((d, cols), single_buffer=True),
        ],
        out_specs=pl.BlockSpec((PROJ_ROWS, cols), lambda i: (i, 0)),
        out_shape=jax.ShapeDtypeStruct((n, cols), F32),
        compiler_params=_params("parallel"),
        name="mixer_in_proj",
    )(x, nw, w)


def _bcast_last_row(v):
    return jnp.broadcast_to(v[SUBLANES - 1:SUBLANES, :], v.shape)


def _s5_build_tables(bmat_ref, ct_ref, pwr_ref, pwi_ref, mt_ref, g_ref, ht_ref):
    ts = S5_TILE_STATES
    bre, bim = bmat_ref[0, :, :ts], bmat_ref[0, :, ts:]
    cre, cim = ct_ref[0, :, :ts], ct_ref[0, :, ts:]
    cmb = jnp.concatenate([cre, -cim], axis=1).astype(BF16)
    zero_block = jnp.zeros((LANES, LANES), BF16)
    for tau in range(S5_T):
        pr, pi = pwr_ref[tau:tau + 1, :], pwi_ref[tau:tau + 1, :]
        bw = jnp.concatenate([bre * pr - bim * pi, bre * pi + bim * pr], axis=1)
        t_in = S5_T - 1 - tau
        g_ref[t_in * LANES:(t_in + 1) * LANES, :] = bw.astype(BF16)
        k_tau = lax.dot_general(bw.astype(BF16), cmb, (((1,), (1,)), ((), ())),
                                preferred_element_type=F32).astype(BF16)
        for t_in in range(S5_T - tau):
            t_out = t_in + tau
            mt_ref[t_in * LANES:(t_in + 1) * LANES, t_out * LANES:(t_out + 1) * LANES] = k_tau
        pr1, pi1 = pwr_ref[tau + 1:tau + 2, :], pwi_ref[tau + 1:tau + 2, :]
        ht_ref[tau * LANES:(tau + 1) * LANES, :] = jnp.concatenate(
            [cre * pr1 - cim * pi1, -(cre * pi1 + cim * pr1)], axis=1).astype(BF16)
    steps_per_block = S5_COL_BLOCK // LANES
    for t_out in range(S5_T):
        for t_in in range(t_out + 1, (t_out // steps_per_block + 1) * steps_per_block):
            mt_ref[t_in * LANES:(t_in + 1) * LANES, t_out * LANES:(t_out + 1) * LANES] = zero_block


def _s5_body(u_ref, bmat_ref, ct_ref, pwr_ref, pwi_ref, tre_ref, tim_ref, d_ref, o_ref,
             mt_ref, g_ref, ht_ref):
    @pl.when(pl.program_id(1) == 0)
    def _():
        _s5_build_tables(bmat_ref, ct_ref, pwr_ref, pwi_ref, mt_ref, g_ref, ht_ref)

    seq = u_ref.shape[0]
    nch = seq // S5_T
    ts = S5_TILE_STATES
    pieces = [u_ref[pl.ds(t, nch, stride=S5_T), :] for t in range(S5_T)]
    ucat = jnp.concatenate(pieces, axis=1).astype(BF16)
    x_end = jnp.dot(ucat, g_ref[...], preferred_element_type=F32)

    row_id = lax.broadcasted_iota(jnp.int32, (SUBLANES, ts), 0)
    cr = jnp.zeros((SUBLANES, ts), F32)
    ci = jnp.zeros((SUBLANES, ts), F32)
    pr = tre_ref[SCAN_CARRY]
    pi = tim_ref[SCAN_CARRY]
    prev_r, prev_i = [], []
    for k in range(nch // SUBLANES):
        rs = slice(k * SUBLANES, (k + 1) * SUBLANES)
        xr = x_end[rs, :ts]
        xi = x_end[rs, ts:]
        for s in range(LOG2_SUBLANES):
            ar = tre_ref[s]
            ai = tim_ref[s]
            sr = pltpu.roll(xr, 1 << s, axis=0)
            si = pltpu.roll(xi, 1 << s, axis=0)
            xr, xi = xr + ar * sr - ai * si, xi + ar * si + ai * sr
        xr, xi = xr + pr * cr - pi * ci, xi + pr * ci + pi * cr
        prev_r.append(jnp.where(row_id == 0, cr, pltpu.roll(xr, 1, axis=0)))
        prev_i.append(jnp.where(row_id == 0, ci, pltpu.roll(xi, 1, axis=0)))
        cr = _bcast_last_row(xr)
        ci = _bcast_last_row(xi)
    xprev = jnp.concatenate(
        [jnp.concatenate(prev_r, axis=0), jnp.concatenate(prev_i, axis=0)], axis=1).astype(BF16)
    y_carry = lax.dot_general(xprev, ht_ref[...], (((1,), (1,)), ((), ())),
                              preferred_element_type=F32)

    steps_per_block = S5_COL_BLOCK // LANES
    for c in range(S5_T // steps_per_block):
        cols = slice(c * S5_COL_BLOCK, (c + 1) * S5_COL_BLOCK)
        k_hi = (c + 1) * S5_COL_BLOCK
        y_c = y_carry[:, cols] + jnp.dot(ucat[:, :k_hi], mt_ref[:k_hi, cols],
                                         preferred_element_type=F32)
        for tt in range(steps_per_block):
            t = c * steps_per_block + tt
            y_t = y_c[:, tt * LANES:(tt + 1) * LANES] + d_ref[...] * pieces[t]
            o_ref[pl.ds(t, nch, stride=S5_T), :] = jax.nn.gelu(y_t)


def _s5(proj, col_block, batch, seq, tabs, d_skip):
    n = proj.shape[0]
    n_tiles = BRANCH_WIDTH // LANES
    tl = S5_T * LANES
    ts = S5_TILE_STATES
    return pl.pallas_call(
        _s5_body,
        grid=(n_tiles, batch),
        in_specs=[
            pl.BlockSpec((seq, LANES), lambda j, b: (b, col_block * n_tiles + j)),
            pl.BlockSpec((1, LANES, 2 * ts), lambda j, b: (j, 0, 0)),
            pl.BlockSpec((1, LANES, 2 * ts), lambda j, b: (j, 0, 0)),
            pl.BlockSpec((S5_T + 1, ts), lambda j, b: (0, j)),
            pl.BlockSpec((S5_T + 1, ts), lambda j, b: (0, j)),
            pl.BlockSpec((SCAN_TABLES, SUBLANES, ts), lambda j, b: (0, 0, j)),
            pl.BlockSpec((SCAN_TABLES, SUBLANES, ts), lambda j, b: (0, 0, j)),
            pl.BlockSpec((1, LANES), lambda j, b: (0, j)),
        ],
        out_specs=pl.BlockSpec((seq, LANES), lambda j, b: (b, j)),
        out_shape=jax.ShapeDtypeStruct((n, BRANCH_WIDTH), F32),
        scratch_shapes=[
            pltpu.VMEM((tl, tl), BF16),
            pltpu.VMEM((tl, 2 * ts), BF16),
            pltpu.VMEM((tl, 2 * ts), BF16),
        ],
        compiler_params=_params("parallel", "arbitrary"),
        name="s5_mixer",
    )(proj, tabs["bmat"], tabs["ct"], tabs["pwr"], tabs["pwi"], tabs["tre"], tabs["tim"], d_skip)


def _rglru_body(x_ref, g_ref, cw_ref, cb_ref, wa_ref, ba_ref, wx_ref, bx_ref, c1_ref, o_ref,
                xbuf_ref, carry_ref):
    rows = x_ref.shape[0]
    first = pl.program_id(1) == 0

    @pl.when(first)
    def _():
        xbuf_ref[0:SUBLANES, :] = jnp.zeros((SUBLANES, x_ref.shape[1]), F32)
        carry_ref[...] = jnp.zeros(carry_ref.shape, F32)

    @pl.when(jnp.logical_not(first))
    def _():
        xbuf_ref[0:SUBLANES, :] = xbuf_ref[rows:rows + SUBLANES, :]

    xbuf_ref[SUBLANES:SUBLANES + rows, :] = x_ref[...]
    xc = cb_ref[...]
    for k in range(CONV_WIDTH):
        off = SUBLANES - (CONV_WIDTH - 1) + k
        xc = xc + cw_ref[k:k + 1, :] * xbuf_ref[off:off + rows, :]
    xcb = xc.astype(BF16)
    r = jax.nn.sigmoid(jnp.dot(xcb, wa_ref[...], preferred_element_type=F32) + ba_ref[...])
    i = jax.nn.sigmoid(jnp.dot(xcb, wx_ref[...], preferred_element_type=F32) + bx_ref[...])
    log_a = c1_ref[...] * r
    a_all = jnp.exp(log_a)
    b_all = jnp.sqrt(1.0 - a_all * a_all) * (i * xc)
    row_id = lax.broadcasted_iota(jnp.int32, (SUBLANES, x_ref.shape[1]), 0)
    carry = carry_ref[...]
    h_blocks = []
    for k in range(rows // SUBLANES):
        rs = slice(k * SUBLANES, (k + 1) * SUBLANES)
        a = a_all[rs]
        b = b_all[rs]
        for s in range(LOG2_SUBLANES):
            d = 1 << s
            m = row_id >= d
            sa = jnp.where(m, pltpu.roll(a, d, axis=0), 1.0)
            sb = jnp.where(m, pltpu.roll(b, d, axis=0), 0.0)
            a, b = a * sa, a * sb + b
        h = a * carry + b
        carry = _bcast_last_row(h)
        h_blocks.append(h)
    carry_ref[...] = carry
    o_ref[...] = jnp.concatenate(h_blocks, axis=0) * jax.nn.gelu(g_ref[...])


def _rglru(proj, x_block, gate_block, batch, seq, conv_w, conv_b, wa, ba, wx, bx, c1):
    n = proj.shape[0]
    bw = BRANCH_WIDTH
    steps = seq // SEQ_ROWS
    row_spec = lambda cb: pl.BlockSpec((SEQ_ROWS, bw), lambda b, l: (b * steps + l, cb))
    return pl.pallas_call(
        _rglru_body,
        grid=(batch, steps),
        in_specs=[
            row_spec(x_block), row_spec(gate_block),
            _const_spec((CONV_WIDTH, bw)), _const_spec((1, bw)),
            _const_spec((bw, bw)), _const_spec((1, bw)),
            _const_spec((bw, bw)), _const_spec((1, bw)),
            _const_spec((1, bw)),
        ],
        out_specs=pl.BlockSpec((SEQ_ROWS, bw), lambda b, l: (b * steps + l, 0)),
        out_shape=jax.ShapeDtypeStruct((n, bw), F32),
        scratch_shapes=[
            pltpu.VMEM((SEQ_ROWS + SUBLANES, bw), F32),
            pltpu.VMEM((SUBLANES, bw), F32),
        ],
        compiler_params=_params("parallel", "arbitrary"),
        name="rglru_mixer",
    )(proj, proj, conv_w, conv_b, wa, ba, wx, bx, c1)


def _rows_bcast(v, row):
    return jnp.broadcast_to(v[row:row + 1, :], (SUBLANES, v.shape[1]))


def _boundary_rows(b, m):
    c, w = b.shape
    if 2 * m >= SUBLANES:
        pieces = []
        for base in range(0, c, 2 * m):
            r = base + m - 1
            pieces.append(jnp.broadcast_to(b[r:r + 1, :], (2 * m, w)))
        return jnp.concatenate(pieces, axis=0)
    sub = lax.broadcasted_iota(jnp.int32, (SUBLANES, w), 0)
    pieces = []
    for base in range(0, c, SUBLANES):
        blk = b[base:base + SUBLANES, :]
        if m == 2:
            out = jnp.where(sub < 4, _rows_bcast(blk, 1), _rows_bcast(blk, 5))
        else:
            out = jnp.where(sub < 2, _rows_bcast(blk, 0),
                            jnp.where(sub < 4, _rows_bcast(blk, 2),
                                      jnp.where(sub < 6, _rows_bcast(blk, 4), _rows_bcast(blk, 6))))
        pieces.append(out)
    return jnp.concatenate(pieces, axis=0)


def _chunk_cumsum(x):
    sub = lax.broadcasted_iota(jnp.int32, (SUBLANES, x.shape[1]), 0)
    blocks = []
    carry = None
    for r in range(0, x.shape[0], SUBLANES):
        v = x[r:r + SUBLANES, :]
        for s in range(LOG2_SUBLANES):
            d = 1 << s
            v = v + jnp.where(sub >= d, pltpu.roll(v, d, axis=0), 0.0)
        if carry is not None:
            v = v + carry
        carry = _bcast_last_row(v)
        blocks.append(v)
    return jnp.concatenate(blocks, axis=0)


def _hgrn2_body(q_ref, f_ref, v_ref, g_ref, lb_ref, nw_ref, o_ref, st_ref):
    @pl.when(pl.program_id(1) == 0)
    def _():
        st_ref[...] = jnp.zeros(st_ref.shape, F32)

    rows = q_ref.shape[0]
    c = HG_CHUNK
    t_id = lax.broadcasted_iota(jnp.int32, (c, c), 0)
    s_id = lax.broadcasted_iota(jnp.int32, (c, c), 1)
    row_id = lax.broadcasted_iota(jnp.int32, (c, HG_DK), 0)
    levels = [1 << e for e in range(int(math.log2(c)))]
    pair_mask = {m: ((t_id >> (e + 1)) == (s_id >> (e + 1))) & ((t_id & m) != 0) & ((s_id & m) == 0)
                 for e, m in enumerate(levels)}
    right_half = {m: (row_id & m) != 0 for m in levels}

    for hd in range(HG_HEADS):
        lanes = slice(hd * HG_DK, (hd + 1) * HG_DK)
        lb = lb_ref[:, lanes]
        st = st_ref[hd]
        for ck in range(rows // c):
            rs = slice(ck * c, (ck + 1) * c)
            qraw = q_ref[rs, lanes]
            z = f_ref[rs, lanes]
            v = v_ref[rs, lanes]
            q = qraw * jax.nn.sigmoid(qraw)
            sig = jax.nn.sigmoid(z)
            f = lb + (1.0 - lb) * sig
            logf = jnp.log(f)
            k = (1.0 - lb) * (1.0 - sig)
            b = _chunk_cumsum(logf)
            vb = v.astype(BF16)
            scores = jnp.zeros((c, c), F32)
            for m in levels:
                rh = right_half[m]
                if m == 1:
                    w = jnp.where(rh, f, 1.0)
                else:
                    d = b - _boundary_rows(b, m)
                    w = jnp.exp(jnp.minimum(d, -d))
                xm = (jnp.where(rh, q, k) * w).astype(BF16)
                part = lax.dot_general(xm, xm, (((1,), (1,)), ((), ())), preferred_element_type=F32)
                scores = jnp.where(pair_mask[m], part, scores)
            o = jnp.dot(scores.astype(BF16), vb, preferred_element_type=F32)
            o = o + jnp.sum(q * k, axis=-1, keepdims=True) * v
            qd = (q * jnp.exp(b)).astype(BF16)
            o = o + lax.dot_general(qd, st.astype(BF16), (((1,), (1,)), ((), ())),
                                    preferred_element_type=F32)
            b_last = b[c - 1:c, :]
            kd = (k * jnp.exp(b_last - b)).astype(BF16)
            st = st * jnp.exp(b_last) + lax.dot_general(
                vb, kd, (((0,), (0,)), ((), ())), preferred_element_type=F32)
            o = o * lax.rsqrt(jnp.mean(o * o, axis=-1, keepdims=True) + EPS) * nw_ref[:, lanes]
            graw = g_ref[rs, lanes]
            o_ref[rs, lanes] = o * (graw * jax.nn.sigmoid(graw))
        st_ref[hd] = st


def _hgrn2(proj, q_block, f_block, v_block, g_block, batch, seq, lb, norm_w):
    n = proj.shape[0]
    bw = BRANCH_WIDTH
    steps = seq // HG_ROWS
    row_spec = lambda cb: pl.BlockSpec((HG_ROWS, bw), lambda b, l: (b * steps + l, cb))
    return pl.pallas_call(
        _hgrn2_body,
        grid=(batch, steps),
        in_specs=[row_spec(q_block), row_spec(f_block), row_spec(v_block), row_spec(g_block),
                  _const_spec((1, bw)), _const_spec((1, bw))],
        out_specs=pl.BlockSpec((HG_ROWS, bw), lambda b, l: (b * steps + l, 0)),
        out_shape=jax.ShapeDtypeStruct((n, bw), F32),
        scratch_shapes=[pltpu.VMEM((HG_HEADS, HG_DV, HG_DK), F32)],
        compiler_params=_params("parallel", "arbitrary"),
        name="hgrn2_mixer",
    )(proj, proj, proj, proj, lb, norm_w)


def _merge_body(x_ref, nw_ref, za_ref, yb_ref, yc_ref, wg_ref, gw_ref, gb_ref, p_ref, wo_ref, o_ref):
    x = x_ref[...]
    d = x.shape[1]
    h = _rmsnorm(x, nw_ref[...]).astype(BF16)
    za = za_ref[...]
    glu = jnp.dot(za.astype(BF16), gw_ref[...], preferred_element_type=F32) + gb_ref[...]
    ya = za * jax.nn.sigmoid(glu)
    merged = None
    for n, y in enumerate((ya, yb_ref[...], yc_ref[...])):
        up = jnp.dot(y.astype(BF16), p_ref[n], preferred_element_type=F32)
        gate = jnp.dot(h, wg_ref[:, n * d:(n + 1) * d], preferred_element_type=F32)
        term = jax.nn.sigmoid(gate) * up
        merged = term if merged is None else merged + term
    o_ref[...] = x + jnp.dot(merged.astype(BF16), wo_ref[...], preferred_element_type=F32)


def _merge(x, nw, za, yb, yc, w_gate, glu_w, glu_b, p, wo):
    n, d = x.shape
    bw = BRANCH_WIDTH
    y_spec = pl.BlockSpec((MERGE_ROWS, bw), lambda i: (i, 0))
    return pl.pallas_call(
        _merge_body,
        grid=(n // MERGE_ROWS,),
        in_specs=[pl.BlockSpec((MERGE_ROWS, d), lambda i: (i, 0)), _const_spec((1, d)),
                  y_spec, y_spec, y_spec,
                  _const_spec((d, N_BRANCH * d), single_buffer=True),
                  _const_spec((bw, bw), single_buffer=True), _const_spec((1, bw)),
                  _const_spec((N_BRANCH, bw, d), single_buffer=True),
                  _const_spec((d, d), single_buffer=True)],
        out_specs=pl.BlockSpec((MERGE_ROWS, d), lambda i: (i, 0)),
        out_shape=jax.ShapeDtypeStruct((n, d), F32),
        compiler_params=_params("parallel"),
        name="merge_out_proj",
    )(x, nw, za, yb, yc, w_gate, glu_w, glu_b, p, wo)


def _s5_tables(lam_re, lam_im, log_dt, b_re, b_im, c_re, c_im):
    groups, states = lam_re.shape
    n_tiles = groups // S5_GROUPS_PER_TILE
    lr = jnp.minimum(lam_re.astype(F32), S5_EIG_MAX)
    li = lam_im.astype(F32)
    dt = jnp.exp(log_dt.astype(F32))[:, None]
    mag = jnp.exp(lr * dt)
    ar = mag * jnp.cos(li * dt)
    ai = mag * jnp.sin(li * dt)
    den = lr * lr + li * li
    fr = ((ar - 1.0) * lr + ai * li) / den
    fi = (ai * lr - (ar - 1.0) * li) / den
    br, bi = b_re.astype(F32), b_im.astype(F32)
    bbr = fr[..., None] * br - fi[..., None] * bi
    bbi = fr[..., None] * bi + fi[..., None] * br
    gpt = S5_GROUPS_PER_TILE
    eye = jnp.eye(gpt, dtype=F32)

    def in_tiles(m):
        m = m.reshape(n_tiles, gpt, states, S5_GROUP)
        return jnp.einsum('jgpc,gh->jgchp', m, eye).reshape(n_tiles, LANES, S5_TILE_STATES)

    def out_tiles(m):
        m = m.reshape(n_tiles, gpt, S5_GROUP, states)
        return jnp.einsum('jgcp,gh->jgchp', m, eye).reshape(n_tiles, LANES, S5_TILE_STATES)

    bmat = jnp.concatenate([in_tiles(bbr), in_tiles(bbi)], axis=2)
    ct = jnp.concatenate([out_tiles(c_re.astype(F32)), out_tiles(c_im.astype(F32))], axis=2)

    def power(kk):
        m = jnp.exp(lr * dt * kk)
        return m * jnp.cos(li * dt * kk), m * jnp.sin(li * dt * kk)

    pwr, pwi = power(jnp.arange(S5_T + 1, dtype=F32)[:, None, None])
    pwr = pwr.reshape(S5_T + 1, groups * states)
    pwi = pwi.reshape(S5_T + 1, groups * states)

    rows = jnp.arange(SUBLANES, dtype=F32)
    tre, tim = [], []
    for s in range(LOG2_SUBLANES):
        d = float(1 << s)
        sr, si = power(jnp.full((SUBLANES, 1, 1), d * S5_T, F32))
        keep = (rows >= d)[:, None, None]
        tre.append(jnp.where(keep, sr, 0.0))
        tim.append(jnp.where(keep, si, 0.0))
    sr, si = power((rows[:, None, None] + 1.0) * S5_T)
    tre.append(sr)
    tim.append(si)
    tre = jnp.stack(tre).reshape(SCAN_TABLES, SUBLANES, groups * states)
    tim = jnp.stack(tim).reshape(SCAN_TABLES, SUBLANES, groups * states)
    return {"bmat": bmat, "ct": ct, "pwr": pwr, "pwi": pwi, "tre": tre, "tim": tim}


def _block_diag(w):
    nb, bi, bo = w.shape
    eye = jnp.eye(nb, dtype=w.dtype)
    return jnp.einsum('hij,hk->hikj', w, eye).reshape(nb * bi, nb * bo)


def kernel(x, norm_w, final_norm_w, ffn_gate, ffn_up, ffn_down, w_in, branch_proj, w_out, s5_lambda_re, s5_lambda_im, s5_log_dt, s5_b_re, s5_b_im, s5_c_re, s5_c_im, s5_d, s5_glu_w, s5_glu_b, hg_lb_logits, hg_norm_w, rg_conv_w, rg_conv_b, rg_wa, rg_ba, rg_wx, rg_bx, rg_lambda):
    batch, seq, d = x.shape
    depth = norm_w.shape[0]
    col = {name: k for k, name in enumerate(("u", "q", "f", "v", "g", "xc", "gc"))}
    mix_cols = len(col) * BRANCH_WIDTH

    p = jax.nn.softmax(hg_lb_logits.astype(F32), axis=0)
    lower_bounds = jnp.cumsum(p, axis=0) - p[0]

    h = x.reshape(batch * seq, d)
    row = lambda v: v.reshape(1, -1).astype(F32)
    for l in range(depth):
        def ffn_w(j):
            return (ffn_gate[l, j].astype(BF16), ffn_up[l, j].astype(BF16),
                    ffn_down[l, j].astype(BF16))

        h = _ffn(h, row(norm_w[l, 0]), *ffn_w(0))
        proj = _proj(h, row(norm_w[l, 1]), w_in[l, :, :mix_cols].astype(BF16))
        tabs = _s5_tables(s5_lambda_re[l], s5_lambda_im[l], s5_log_dt[l], s5_b_re[l], s5_b_im[l],
                          s5_c_re[l], s5_c_im[l])
        za = _s5(proj, col["u"], batch, seq, tabs, row(s5_d[l]))
        yb = _hgrn2(proj, col["q"], col["f"], col["v"], col["g"], batch, seq,
                    row(lower_bounds[l]), row(hg_norm_w[l]))
        c1 = -RG_C * jax.nn.softplus(-rg_lambda[l].astype(F32))
        yc = _rglru(proj, col["xc"], col["gc"], batch, seq, rg_conv_w[l].astype(F32),
                    row(rg_conv_b[l]), _block_diag(rg_wa[l]).astype(BF16), row(rg_ba[l]),
                    _block_diag(rg_wx[l]).astype(BF16), row(rg_bx[l]), row(c1))
        h = _merge(h, row(norm_w[l, 1]), za, yb, yc, w_in[l, :, mix_cols:].astype(BF16),
                   s5_glu_w[l].astype(BF16), row(s5_glu_b[l]), branch_proj[l].astype(BF16),
                   w_out[l].astype(BF16))
        last = l == depth - 1
        h = _ffn(h, row(norm_w[l, 2]), *ffn_w(1), final_w=row(final_norm_w) if last else None)
    return h.reshape(batch, seq, d)
```

```python
import functools
import math

import jax
import jax.numpy as jnp
from jax import lax
from jax.experimental import pallas as pl
from jax.experimental.pallas import tpu as pltpu

F32 = jnp.float32
BF16 = jnp.bfloat16

LANES = 128
SUBLANES = 8
LOG2_SUBLANES = 3
MXU_WIDTH = 256
VMEM_LIMIT_BYTES = 56 * 1024 * 1024

SCAN_CARRY = LOG2_SUBLANES
SCAN_TABLES = LOG2_SUBLANES + 1

EPS = 1e-6
BRANCH_WIDTH = 512
N_BRANCH = 3
S5_GROUP = 16
S5_STATE = 64
S5_EIG_MAX = -1e-4
S5_GROUPS_PER_TILE = LANES // S5_GROUP
S5_TILE_STATES = S5_GROUPS_PER_TILE * S5_STATE
S5_T = 8
S5_COL_BLOCK = MXU_WIDTH
HG_HEADS = 4
HG_DK = 128
HG_DV = 128
HG_CHUNK = 128
RG_BLOCKS = 8
RG_BLOCK = 64
RG_C = 8.0
CONV_WIDTH = 4

FFN_ROWS = 512
FFN_FCHUNK = 256
PROJ_ROWS = 512
SEQ_ROWS = 1024
HG_ROWS = 512
MERGE_ROWS = 512


def _rmsnorm(x, w):
    ms = jnp.mean(x * x, axis=-1, keepdims=True)
    return x * lax.rsqrt(ms + EPS) * w


def _const_spec(shape, single_buffer=False):
    nd = len(shape)
    kw = {"pipeline_mode": pl.Buffered(1)} if single_buffer else {}
    return pl.BlockSpec(shape, lambda *_: (0,) * nd, **kw)


def _stacked_spec(arr, lead, block_tail=None, tail_index=None):
    tail = tuple(arr.shape[len(lead):]) if block_tail is None else tuple(block_tail)
    idx = tuple(lead) + (tuple(tail_index) if tail_index is not None else (0,) * len(tail))
    return pl.BlockSpec((None,) * len(lead) + tail, lambda *_: idx, pipeline_mode=pl.Buffered(1))


def _params(*sem):
    return pltpu.CompilerParams(dimension_semantics=sem, vmem_limit_bytes=VMEM_LIMIT_BYTES)


def _ffn_body(x_ref, nw_ref, wg_ref, wu_ref, wd_ref, *rest, final):
    o_ref = rest[-1]
    x = x_ref[...]
    h = _rmsnorm(x, nw_ref[...]).astype(BF16)
    acc = jnp.zeros(x.shape, F32)
    for c in range(wg_ref.shape[1] // FFN_FCHUNK):
        cols = slice(c * FFN_FCHUNK, (c + 1) * FFN_FCHUNK)
        g = jnp.dot(h, wg_ref[:, cols], preferred_element_type=F32)
        u = jnp.dot(h, wu_ref[:, cols], preferred_element_type=F32)
        a = (g * jax.nn.sigmoid(g) * u).astype(BF16)
        acc = acc + jnp.dot(a, wd_ref[cols, :], preferred_element_type=F32)
    y = x + 0.5 * acc
    if final:
        y = _rmsnorm(y, rest[0][...])
    o_ref[...] = y


def _ffn(x, nw, wg, wu, wd, which, final_w=None):
    n, d = x.shape
    final = final_w is not None
    in_specs = [
        pl.BlockSpec((FFN_ROWS, d), lambda i: (i, 0)),
        _const_spec((1, d)),
        _stacked_spec(wg, which),
        _stacked_spec(wu, which),
        _stacked_spec(wd, which),
    ]
    args = [x, nw, wg, wu, wd]
    if final:
        in_specs.append(_const_spec((1, d)))
        args.append(final_w)
    return pl.pallas_call(
        functools.partial(_ffn_body, final=final),
        grid=(n // FFN_ROWS,),
        in_specs=in_specs,
        out_specs=pl.BlockSpec((FFN_ROWS, d), lambda i: (i, 0)),
        out_shape=jax.ShapeDtypeStruct((n, d), F32),
        compiler_params=_params("parallel"),
        name="ffn_final" if final else "ffn",
    )(*args)


def _proj_body(x_ref, nw_ref, w_ref, o_ref):
    h = _rmsnorm(x_ref[...], nw_ref[...]).astype(BF16)
    o_ref[...] = jnp.dot(h, w_ref[...], preferred_element_type=F32)


def _proj(x, nw, w_in, layer, cols):
    n, d = x.shape
    return pl.pallas_call(
        _proj_body,
        grid=(n // PROJ_ROWS,),
        in_specs=[
            pl.BlockSpec((PROJ_ROWS, d), lambda i: (i, 0)),
            _const_spec((1, d)),
            _stacked_spec(w_in, (layer,), block_tail=(d, cols), tail_index=(0, 0)),
        ],
        out_specs=pl.BlockSpec((PROJ_ROWS, cols), lambda i: (i, 0)),
        out_shape=jax.ShapeDtypeStruct((n, cols), F32),
        compiler_params=_params("parallel"),
        name="mixer_in_proj",
    )(x, nw, w_in)


def _bcast_last_row(v):
    return jnp.broadcast_to(v[SUBLANES - 1:SUBLANES, :], v.shape)


def _s5_build_tables(bmat_ref, ct_ref, pwr_ref, pwi_ref, mt_ref, g_ref, ht_ref):
    ts = S5_TILE_STATES
    bre, bim = bmat_ref[0, :, :ts], bmat_ref[0, :, ts:]
    cre, cim = ct_ref[0, :, :ts], ct_ref[0, :, ts:]
    cmb = jnp.concatenate([cre, -cim], axis=1).astype(BF16)
    zero_block = jnp.zeros((LANES, LANES), BF16)
    for tau in range(S5_T):
        pr, pi = pwr_ref[tau:tau + 1, :], pwi_ref[tau:tau + 1, :]
        bw = jnp.concatenate([bre * pr - bim * pi, bre * pi + bim * pr], axis=1)
        t_in = S5_T - 1 - tau
        g_ref[t_in * LANES:(t_in + 1) * LANES, :] = bw.astype(BF16)
        k_tau = lax.dot_general(bw.astype(BF16), cmb, (((1,), (1,)), ((), ())),
                                preferred_element_type=F32).astype(BF16)
        for t_in in range(S5_T - tau):
            t_out = t_in + tau
            mt_ref[t_in * LANES:(t_in + 1) * LANES, t_out * LANES:(t_out + 1) * LANES] = k_tau
        pr1, pi1 = pwr_ref[tau + 1:tau + 2, :], pwi_ref[tau + 1:tau + 2, :]
        ht_ref[tau * LANES:(tau + 1) * LANES, :] = jnp.concatenate(
            [cre * pr1 - cim * pi1, -(cre * pi1 + cim * pr1)], axis=1).astype(BF16)
    steps_per_block = S5_COL_BLOCK // LANES
    for t_out in range(S5_T):
        for t_in in range(t_out + 1, (t_out // steps_per_block + 1) * steps_per_block):
            mt_ref[t_in * LANES:(t_in + 1) * LANES, t_out * LANES:(t_out + 1) * LANES] = zero_block


def _s5_body(u_ref, bmat_ref, ct_ref, pwr_ref, pwi_ref, tre_ref, tim_ref, d_ref, o_ref,
             mt_ref, g_ref, ht_ref):
    @pl.when(pl.program_id(1) == 0)
    def _():
        _s5_build_tables(bmat_ref, ct_ref, pwr_ref, pwi_ref, mt_ref, g_ref, ht_ref)

    seq = u_ref.shape[0]
    nch = seq // S5_T
    ts = S5_TILE_STATES
    pieces = [u_ref[pl.ds(t, nch, stride=S5_T), :] for t in range(S5_T)]
    ucat = jnp.concatenate(pieces, axis=1).astype(BF16)
    x_end = jnp.dot(ucat, g_ref[...], preferred_element_type=F32)

    row_id = lax.broadcasted_iota(jnp.int32, (SUBLANES, ts), 0)
    cr = jnp.zeros((SUBLANES, ts), F32)
    ci = jnp.zeros((SUBLANES, ts), F32)
    pr = tre_ref[SCAN_CARRY]
    pi = tim_ref[SCAN_CARRY]
    prev_r, prev_i = [], []
    for k in range(nch // SUBLANES):
        rs = slice(k * SUBLANES, (k + 1) * SUBLANES)
        xr = x_end[rs, :ts]
        xi = x_end[rs, ts:]
        for s in range(LOG2_SUBLANES):
            ar = tre_ref[s]
            ai = tim_ref[s]
            sr = pltpu.roll(xr, 1 << s, axis=0)
            si = pltpu.roll(xi, 1 << s, axis=0)
            xr, xi = xr + ar * sr - ai * si, xi + ar * si + ai * sr
        xr, xi = xr + pr * cr - pi * ci, xi + pr * ci + pi * cr
        prev_r.append(jnp.where(row_id == 0, cr, pltpu.roll(xr, 1, axis=0)))
        prev_i.append(jnp.where(row_id == 0, ci, pltpu.roll(xi, 1, axis=0)))
        cr = _bcast_last_row(xr)
        ci = _bcast_last_row(xi)
    xprev = jnp.concatenate(
        [jnp.concatenate(prev_r, axis=0), jnp.concatenate(prev_i, axis=0)], axis=1).astype(BF16)
    y_carry = lax.dot_general(xprev, ht_ref[...], (((1,), (1,)), ((), ())),
                              preferred_element_type=F32)

    steps_per_block = S5_COL_BLOCK // LANES
    for c in range(S5_T // steps_per_block):
        cols = slice(c * S5_COL_BLOCK, (c + 1) * S5_COL_BLOCK)
        k_hi = (c + 1) * S5_COL_BLOCK
        y_c = y_carry[:, cols] + jnp.dot(ucat[:, :k_hi], mt_ref[:k_hi, cols],
                                         preferred_element_type=F32)
        for tt in range(steps_per_block):
            t = c * steps_per_block + tt
            y_t = y_c[:, tt * LANES:(tt + 1) * LANES] + d_ref[...] * pieces[t]
            o_ref[pl.ds(t, nch, stride=S5_T), :] = jax.nn.gelu(y_t)


def _s5(proj, col_block, batch, seq, tabs, d_skip):
    n = proj.shape[0]
    n_tiles = BRANCH_WIDTH // LANES
    tl = S5_T * LANES
    ts = S5_TILE_STATES
    return pl.pallas_call(
        _s5_body,
        grid=(n_tiles, batch),
        in_specs=[
            pl.BlockSpec((seq, LANES), lambda j, b: (b, col_block * n_tiles + j)),
            pl.BlockSpec((1, LANES, 2 * ts), lambda j, b: (j, 0, 0)),
            pl.BlockSpec((1, LANES, 2 * ts), lambda j, b: (j, 0, 0)),
            pl.BlockSpec((S5_T + 1, ts), lambda j, b: (0, j)),
            pl.BlockSpec((S5_T + 1, ts), lambda j, b: (0, j)),
            pl.BlockSpec((SCAN_TABLES, SUBLANES, ts), lambda j, b: (0, 0, j)),
            pl.BlockSpec((SCAN_TABLES, SUBLANES, ts), lambda j, b: (0, 0, j)),
            pl.BlockSpec((1, LANES), lambda j, b: (0, j)),
        ],
        out_specs=pl.BlockSpec((seq, LANES), lambda j, b: (b, j)),
        out_shape=jax.ShapeDtypeStruct((n, BRANCH_WIDTH), F32),
        scratch_shapes=[
            pltpu.VMEM((tl, tl), BF16),
            pltpu.VMEM((tl, 2 * ts), BF16),
            pltpu.VMEM((tl, 2 * ts), BF16),
        ],
        compiler_params=_params("parallel", "arbitrary"),
        name="s5_mixer",
    )(proj, tabs["bmat"], tabs["ct"], tabs["pwr"], tabs["pwi"], tabs["tre"], tabs["tim"], d_skip)


def _rglru_body(x_ref, g_ref, cw_ref, cb_ref, wa_ref, ba_ref, wx_ref, bx_ref, c1_ref, o_ref,
                xbuf_ref, carry_ref):
    rows = x_ref.shape[0]
    first = pl.program_id(1) == 0

    @pl.when(first)
    def _():
        xbuf_ref[0:SUBLANES, :] = jnp.zeros((SUBLANES, x_ref.shape[1]), F32)
        carry_ref[...] = jnp.zeros(carry_ref.shape, F32)

    @pl.when(jnp.logical_not(first))
    def _():
        xbuf_ref[0:SUBLANES, :] = xbuf_ref[rows:rows + SUBLANES, :]

    xbuf_ref[SUBLANES:SUBLANES + rows, :] = x_ref[...]
    xc = cb_ref[...]
    for k in range(CONV_WIDTH):
        off = SUBLANES - (CONV_WIDTH - 1) + k
        xc = xc + cw_ref[k:k + 1, :] * xbuf_ref[off:off + rows, :]
    xcb = xc.astype(BF16)
    r = jax.nn.sigmoid(jnp.dot(xcb, wa_ref[...], preferred_element_type=F32) + ba_ref[...])
    i = jax.nn.sigmoid(jnp.dot(xcb, wx_ref[...], preferred_element_type=F32) + bx_ref[...])
    log_a = c1_ref[...] * r
    a_all = jnp.exp(log_a)
    b_all = jnp.sqrt(1.0 - a_all * a_all) * (i * xc)
    row_id = lax.broadcasted_iota(jnp.int32, (SUBLANES, x_ref.shape[1]), 0)
    carry = carry_ref[...]
    h_blocks = []
    for k in range(rows // SUBLANES):
        rs = slice(k * SUBLANES, (k + 1) * SUBLANES)
        a = a_all[rs]
        b = b_all[rs]
        for s in range(LOG2_SUBLANES):
            d = 1 << s
            m = row_id >= d
            sa = jnp.where(m, pltpu.roll(a, d, axis=0), 1.0)
            sb = jnp.where(m, pltpu.roll(b, d, axis=0), 0.0)
            a, b = a * sa, a * sb + b
        h = a * carry + b
        carry = _bcast_last_row(h)
        h_blocks.append(h)
    carry_ref[...] = carry
    o_ref[...] = jnp.concatenate(h_blocks, axis=0) * jax.nn.gelu(g_ref[...])


def _rglru(proj, x_block, gate_block, batch, seq, conv_w, conv_b, wa, ba, wx, bx, c1):
    n = proj.shape[0]
    bw = BRANCH_WIDTH
    steps = seq // SEQ_ROWS
    row_spec = lambda cb: pl.BlockSpec((SEQ_ROWS, bw), lambda b, l: (b * steps + l, cb))
    return pl.pallas_call(
        _rglru_body,
        grid=(batch, steps),
        in_specs=[
            row_spec(x_block), row_spec(gate_block),
            _const_spec((CONV_WIDTH, bw)), _const_spec((1, bw)),
            _const_spec((bw, bw)), _const_spec((1, bw)),
            _const_spec((bw, bw)), _const_spec((1, bw)),
            _const_spec((1, bw)),
        ],
        out_specs=pl.BlockSpec((SEQ_ROWS, bw), lambda b, l: (b * steps + l, 0)),
        out_shape=jax.ShapeDtypeStruct((n, bw), F32),
        scratch_shapes=[
            pltpu.VMEM((SEQ_ROWS + SUBLANES, bw), F32),
            pltpu.VMEM((SUBLANES, bw), F32),
        ],
        compiler_params=_params("parallel", "arbitrary"),
        name="rglru_mixer",
    )(proj, proj, conv_w, conv_b, wa, ba, wx, bx, c1)


def _rows_bcast(v, row):
    return jnp.broadcast_to(v[row:row + 1, :], (SUBLANES, v.shape[1]))


def _boundary_rows(b, m):
    c, w = b.shape
    if 2 * m >= SUBLANES:
        pieces = []
        for base in range(0, c, 2 * m):
            r = base + m - 1
            pieces.append(jnp.broadcast_to(b[r:r + 1, :], (2 * m, w)))
        return jnp.concatenate(pieces, axis=0)
    sub = lax.broadcasted_iota(jnp.int32, (SUBLANES, w), 0)
    pieces = []
    for base in range(0, c, SUBLANES):
        blk = b[base:base + SUBLANES, :]
        if m == 2:
            out = jnp.where(sub < 4, _rows_bcast(blk, 1), _rows_bcast(blk, 5))
        else:
            out = jnp.where(sub < 2, _rows_bcast(blk, 0),
                            jnp.where(sub < 4, _rows_bcast(blk, 2),
                                      jnp.where(sub < 6, _rows_bcast(blk, 4), _rows_bcast(blk, 6))))
        pieces.append(out)
    return jnp.concatenate(pieces, axis=0)


def _chunk_cumsum(x):
    sub = lax.broadcasted_iota(jnp.int32, (SUBLANES, x.shape[1]), 0)
    blocks = []
    carry = None
    for r in range(0, x.shape[0], SUBLANES):
        v = x[r:r + SUBLANES, :]
        for s in range(LOG2_SUBLANES):
            d = 1 << s
            v = v + jnp.where(sub >= d, pltpu.roll(v, d, axis=0), 0.0)
        if carry is not None:
            v = v + carry
        carry = _bcast_last_row(v)
        blocks.append(v)
    return jnp.concatenate(blocks, axis=0)


def _hgrn2_body(q_ref, f_ref, v_ref, g_ref, lb_ref, nw_ref, o_ref, st_ref):
    @pl.when(pl.program_id(1) == 0)
    def _():
        st_ref[...] = jnp.zeros(st_ref.shape, F32)

    rows = q_ref.shape[0]
    c = HG_CHUNK
    t_id = lax.broadcasted_iota(jnp.int32, (c, c), 0)
    s_id = lax.broadcasted_iota(jnp.int32, (c, c), 1)
    row_id = lax.broadcasted_iota(jnp.int32, (c, HG_DK), 0)
    levels = [1 << e for e in range(int(math.log2(c)))]
    pair_mask = {m: ((t_id >> (e + 1)) == (s_id >> (e + 1))) & ((t_id & m) != 0) & ((s_id & m) == 0)
                 for e, m in enumerate(levels)}
    right_half = {m: (row_id & m) != 0 for m in levels}

    for hd in range(HG_HEADS):
        lanes = slice(hd * HG_DK, (hd + 1) * HG_DK)
        lb = lb_ref[:, lanes]
        st = st_ref[hd]
        for ck in range(rows // c):
            rs = slice(ck * c, (ck + 1) * c)
            qraw = q_ref[rs, lanes]
            z = f_ref[rs, lanes]
            v = v_ref[rs, lanes]
            q = qraw * jax.nn.sigmoid(qraw)
            sig = jax.nn.sigmoid(z)
            f = lb + (1.0 - lb) * sig
            logf = jnp.log(f)
            k = (1.0 - lb) * (1.0 - sig)
            b = _chunk_cumsum(logf)
            vb = v.astype(BF16)
            scores = jnp.zeros((c, c), F32)
            for m in levels:
                rh = right_half[m]
                if m == 1:
                    w = jnp.where(rh, f, 1.0)
                else:
                    d = b - _boundary_rows(b, m)
                    w = jnp.exp(jnp.minimum(d, -d))
                xm = (jnp.where(rh, q, k) * w).astype(BF16)
                part = lax.dot_general(xm, xm, (((1,), (1,)), ((), ())), preferred_element_type=F32)
                scores = jnp.where(pair_mask[m], part, scores)
            o = jnp.dot(scores.astype(BF16), vb, preferred_element_type=F32)
            o = o + jnp.sum(q * k, axis=-1, keepdims=True) * v
            qd = (q * jnp.exp(b)).astype(BF16)
            o = o + lax.dot_general(qd, st.astype(BF16), (((1,), (1,)), ((), ())),
                                    preferred_element_type=F32)
            b_last = b[c - 1:c, :]
            kd = (k * jnp.exp(b_last - b)).astype(BF16)
            st = st * jnp.exp(b_last) + lax.dot_general(
                vb, kd, (((0,), (0,)), ((), ())), preferred_element_type=F32)
            o = o * lax.rsqrt(jnp.mean(o * o, axis=-1, keepdims=True) + EPS) * nw_ref[:, lanes]
            graw = g_ref[rs, lanes]
            o_ref[rs, lanes] = o * (graw * jax.nn.sigmoid(graw))
        st_ref[hd] = st


def _hgrn2(proj, q_block, f_block, v_block, g_block, batch, seq, lb, norm_w):
    n = proj.shape[0]
    bw = BRANCH_WIDTH
    steps = seq // HG_ROWS
    row_spec = lambda cb: pl.BlockSpec((HG_ROWS, bw), lambda b, l: (b * steps + l, cb))
    return pl.pallas_call(
        _hgrn2_body,
        grid=(batch, steps),
        in_specs=[row_spec(q_block), row_spec(f_block), row_spec(v_block), row_spec(g_block),
                  _const_spec((1, bw)), _const_spec((1, bw))],
        out_specs=pl.BlockSpec((HG_ROWS, bw), lambda b, l: (b * steps + l, 0)),
        out_shape=jax.ShapeDtypeStruct((n, bw), F32),
        scratch_shapes=[pltpu.VMEM((HG_HEADS, HG_DV, HG_DK), F32)],
        compiler_params=_params("parallel", "arbitrary"),
        name="hgrn2_mixer",
    )(proj, proj, proj, proj, lb, norm_w)


def _merge_body(x_ref, nw_ref, za_ref, yb_ref, yc_ref, *rest):
    wg_refs, (gw_ref, gb_ref, p_ref, wo_ref, o_ref) = rest[:-5], rest[-5:]
    per_gate = len(wg_refs) // N_BRANCH
    x = x_ref[...]
    h = _rmsnorm(x, nw_ref[...]).astype(BF16)
    za = za_ref[...]
    glu = jnp.dot(za.astype(BF16), gw_ref[...], preferred_element_type=F32) + gb_ref[...]
    ya = za * jax.nn.sigmoid(glu)
    merged = None
    for n, y in enumerate((ya, yb_ref[...], yc_ref[...])):
        up = jnp.dot(y.astype(BF16), p_ref[n], preferred_element_type=F32)
        gate = jnp.concatenate(
            [jnp.dot(h, w[...], preferred_element_type=F32)
             for w in wg_refs[n * per_gate:(n + 1) * per_gate]], axis=1)
        term = jax.nn.sigmoid(gate) * up
        merged = term if merged is None else merged + term
    o_ref[...] = x + jnp.dot(merged.astype(BF16), wo_ref[...], preferred_element_type=F32)


def _merge(x, nw, za, yb, yc, w_in, first_gate_block, glu_w, glu_b, p, wo, layer):
    n, d = x.shape
    bw = BRANCH_WIDTH
    n_gate_blocks = N_BRANCH * d // bw
    y_spec = pl.BlockSpec((MERGE_ROWS, bw), lambda i: (i, 0))
    gate_specs = [_stacked_spec(w_in, (layer,), block_tail=(d, bw), tail_index=(0, first_gate_block + k))
                  for k in range(n_gate_blocks)]
    return pl.pallas_call(
        _merge_body,
        grid=(n // MERGE_ROWS,),
        in_specs=[pl.BlockSpec((MERGE_ROWS, d), lambda i: (i, 0)), _const_spec((1, d)),
                  y_spec, y_spec, y_spec] + gate_specs
        + [_stacked_spec(glu_w, (layer,)), _const_spec((1, bw)),
           _stacked_spec(p, (layer,)), _stacked_spec(wo, (layer,))],
        out_specs=pl.BlockSpec((MERGE_ROWS, d), lambda i: (i, 0)),
        out_shape=jax.ShapeDtypeStruct((n, d), F32),
        compiler_params=_params("parallel"),
        name="merge_out_proj",
    )(x, nw, za, yb, yc, *([w_in] * n_gate_blocks), glu_w, glu_b, p, wo)


def _s5_tables(lam_re, lam_im, log_dt, b_re, b_im, c_re, c_im):
    groups, states = lam_re.shape
    n_tiles = groups // S5_GROUPS_PER_TILE
    lr = jnp.minimum(lam_re.astype(F32), S5_EIG_MAX)
    li = lam_im.astype(F32)
    dt = jnp.exp(log_dt.astype(F32))[:, None]
    mag = jnp.exp(lr * dt)
    ar = mag * jnp.cos(li * dt)
    ai = mag * jnp.sin(li * dt)
    den = lr * lr + li * li
    fr = ((ar - 1.0) * lr + ai * li) / den
    fi = (ai * lr - (ar - 1.0) * li) / den
    br, bi = b_re.astype(F32), b_im.astype(F32)
    bbr = fr[..., None] * br - fi[..., None] * bi
    bbi = fr[..., None] * bi + fi[..., None] * br
    gpt = S5_GROUPS_PER_TILE
    eye = jnp.eye(gpt, dtype=F32)

    def in_tiles(m):
        m = m.reshape(n_tiles, gpt, states, S5_GROUP)
        return jnp.einsum('jgpc,gh->jgchp', m, eye).reshape(n_tiles, LANES, S5_TILE_STATES)

    def out_tiles(m):
        m = m.reshape(n_tiles, gpt, S5_GROUP, states)
        return jnp.einsum('jgcp,gh->jgchp', m, eye).reshape(n_tiles, LANES, S5_TILE_STATES)

    bmat = jnp.concatenate([in_tiles(bbr), in_tiles(bbi)], axis=2)
    ct = jnp.concatenate([out_tiles(c_re.astype(F32)), out_tiles(c_im.astype(F32))], axis=2)

    def power(kk):
        m = jnp.exp(lr * dt * kk)
        return m * jnp.cos(li * dt * kk), m * jnp.sin(li * dt * kk)

    pwr, pwi = power(jnp.arange(S5_T + 1, dtype=F32)[:, None, None])
    pwr = pwr.reshape(S5_T + 1, groups * states)
    pwi = pwi.reshape(S5_T + 1, groups * states)

    rows = jnp.arange(SUBLANES, dtype=F32)
    tre, tim = [], []
    for s in range(LOG2_SUBLANES):
        d = float(1 << s)
        sr, si = power(jnp.full((SUBLANES, 1, 1), d * S5_T, F32))
        keep = (rows >= d)[:, None, None]
        tre.append(jnp.where(keep, sr, 0.0))
        tim.append(jnp.where(keep, si, 0.0))
    sr, si = power((rows[:, None, None] + 1.0) * S5_T)
    tre.append(sr)
    tim.append(si)
    tre = jnp.stack(tre).reshape(SCAN_TABLES, SUBLANES, groups * states)
    tim = jnp.stack(tim).reshape(SCAN_TABLES, SUBLANES, groups * states)
    return {"bmat": bmat, "ct": ct, "pwr": pwr, "pwi": pwi, "tre": tre, "tim": tim}


def _block_diag(w):
    nb, bi, bo = w.shape
    eye = jnp.eye(nb, dtype=w.dtype)
    return jnp.einsum('hij,hk->hikj', w, eye).reshape(nb * bi, nb * bo)


def kernel(x, norm_w, final_norm_w, ffn_gate, ffn_up, ffn_down, w_in, branch_proj, w_out, s5_lambda_re, s5_lambda_im, s5_log_dt, s5_b_re, s5_b_im, s5_c_re, s5_c_im, s5_d, s5_glu_w, s5_glu_b, hg_lb_logits, hg_norm_w, rg_conv_w, rg_conv_b, rg_wa, rg_ba, rg_wx, rg_bx, rg_lambda):
    batch, seq, d = x.shape
    depth = norm_w.shape[0]
    col = {name: k for k, name in enumerate(("u", "q", "f", "v", "g", "xc", "gc"))}
    mix_cols = len(col) * BRANCH_WIDTH

    p = jax.nn.softmax(hg_lb_logits.astype(F32), axis=0)
    lower_bounds = jnp.cumsum(p, axis=0) - p[0]

    ffn_w = (ffn_gate.astype(BF16), ffn_up.astype(BF16), ffn_down.astype(BF16))
    w_in_b = w_in.astype(BF16)
    glu_w_b, branch_proj_b, w_out_b = (s5_glu_w.astype(BF16), branch_proj.astype(BF16),
                                       w_out.astype(BF16))

    h = x.reshape(batch * seq, d)
    row = lambda v: v.reshape(1, -1).astype(F32)
    for l in range(depth):
        h = _ffn(h, row(norm_w[l, 0]), *ffn_w, (l, 0))
        proj = _proj(h, row(norm_w[l, 1]), w_in_b, l, mix_cols)
        tabs = _s5_tables(s5_lambda_re[l], s5_lambda_im[l], s5_log_dt[l], s5_b_re[l], s5_b_im[l],
                          s5_c_re[l], s5_c_im[l])
        za = _s5(proj, col["u"], batch, seq, tabs, row(s5_d[l]))
        yb = _hgrn2(proj, col["q"], col["f"], col["v"], col["g"], batch, seq,
                    row(lower_bounds[l]), row(hg_norm_w[l]))
        c1 = -RG_C * jax.nn.softplus(-rg_lambda[l].astype(F32))
        yc = _rglru(proj, col["xc"], col["gc"], batch, seq, rg_conv_w[l].astype(F32),
                    row(rg_conv_b[l]), _block_diag(rg_wa[l]).astype(BF16), row(rg_ba[l]),
                    _block_diag(rg_wx[l]).astype(BF16), row(rg_bx[l]), row(c1))
        h = _merge(h, row(norm_w[l, 1]), za, yb, yc, w_in_b, len(col), glu_w_b,
                   row(s5_glu_b[l]), branch_proj_b, w_out_b, l)
        last = l == depth - 1
        h = _ffn(h, row(norm_w[l, 2]), *ffn_w, (l, 1),
                 final_w=row(final_norm_w) if last else None)
    return h.reshape(batch, seq, d)
```

```python
import functools
import math

import jax
import jax.numpy as jnp
from jax import lax
from jax.experimental import pallas as pl
from jax.experimental.pallas import tpu as pltpu

F32 = jnp.float32
BF16 = jnp.bfloat16

LANES = 128
SUBLANES = 8
LOG2_SUBLANES = 3
MXU_WIDTH = 256
VMEM_LIMIT_BYTES = 56 * 1024 * 1024

SCAN_CARRY = LOG2_SUBLANES
SCAN_TABLES = LOG2_SUBLANES + 1

EPS = 1e-6
BRANCH_WIDTH = 512
N_BRANCH = 3
S5_GROUP = 16
S5_STATE = 64
S5_EIG_MAX = -1e-4
S5_GROUPS_PER_TILE = LANES // S5_GROUP
S5_TILE_STATES = S5_GROUPS_PER_TILE * S5_STATE
S5_T = 8
S5_COL_BLOCK = MXU_WIDTH
HG_HEADS = 4
HG_DK = 128
HG_DV = 128
HG_CHUNK = 128
RG_BLOCKS = 8
RG_BLOCK = 64
RG_C = 8.0
CONV_WIDTH = 4

FFN_ROWS = 512
FFN_FCHUNK = 256
PROJ_ROWS = 512
SEQ_ROWS = 1024
HG_ROWS = 512
MERGE_ROWS = 512


def _rmsnorm(x, w):
    ms = jnp.mean(x * x, axis=-1, keepdims=True)
    return x * lax.rsqrt(ms + EPS) * w


def _const_spec(shape, single_buffer=False):
    nd = len(shape)
    kw = {"pipeline_mode": pl.Buffered(1)} if single_buffer else {}
    return pl.BlockSpec(shape, lambda *_: (0,) * nd, **kw)


def _stacked_spec(arr, lead, block_tail=None, tail_index=None):
    tail = tuple(arr.shape[len(lead):]) if block_tail is None else tuple(block_tail)
    idx = tuple(lead) + (tuple(tail_index) if tail_index is not None else (0,) * len(tail))
    return pl.BlockSpec((None,) * len(lead) + tail, lambda *_: idx, pipeline_mode=pl.Buffered(1))


def _params(*sem):
    return pltpu.CompilerParams(dimension_semantics=sem, vmem_limit_bytes=VMEM_LIMIT_BYTES)


def _ffn_body(x_ref, nw_ref, wg_ref, wu_ref, wd_ref, *rest, final):
    o_ref = rest[-1]
    x = x_ref[...]
    h = _rmsnorm(x, nw_ref[...]).astype(BF16)
    acc = jnp.zeros(x.shape, F32)
    for c in range(wg_ref.shape[1] // FFN_FCHUNK):
        cols = slice(c * FFN_FCHUNK, (c + 1) * FFN_FCHUNK)
        g = jnp.dot(h, wg_ref[:, cols].astype(BF16), preferred_element_type=F32)
        u = jnp.dot(h, wu_ref[:, cols].astype(BF16), preferred_element_type=F32)
        a = (g * jax.nn.sigmoid(g) * u).astype(BF16)
        acc = acc + jnp.dot(a, wd_ref[cols, :].astype(BF16), preferred_element_type=F32)
    y = x + 0.5 * acc
    if final:
        y = _rmsnorm(y, rest[0][...])
    o_ref[...] = y


def _ffn(x, nw, wg, wu, wd, which, final_w=None):
    n, d = x.shape
    final = final_w is not None
    in_specs = [
        pl.BlockSpec((FFN_ROWS, d), lambda i: (i, 0)),
        _const_spec((1, d)),
        _stacked_spec(wg, which),
        _stacked_spec(wu, which),
        _stacked_spec(wd, which),
    ]
    args = [x, nw, wg, wu, wd]
    if final:
        in_specs.append(_const_spec((1, d)))
        args.append(final_w)
    return pl.pallas_call(
        functools.partial(_ffn_body, final=final),
        grid=(n // FFN_ROWS,),
        in_specs=in_specs,
        out_specs=pl.BlockSpec((FFN_ROWS, d), lambda i: (i, 0)),
        out_shape=jax.ShapeDtypeStruct((n, d), F32),
        compiler_params=_params("parallel"),
        name="ffn_final" if final else "ffn",
    )(*args)


def _proj_body(x_ref, nw_ref, w_ref, o_ref):
    h = _rmsnorm(x_ref[...], nw_ref[...]).astype(BF16)
    o_ref[...] = jnp.dot(h, w_ref[...].astype(BF16), preferred_element_type=F32)


def _proj(x, nw, w_in, layer, cols):
    n, d = x.shape
    return pl.pallas_call(
        _proj_body,
        grid=(n // PROJ_ROWS,),
        in_specs=[
            pl.BlockSpec((PROJ_ROWS, d), lambda i: (i, 0)),
            _const_spec((1, d)),
            _stacked_spec(w_in, (layer,), block_tail=(d, cols), tail_index=(0, 0)),
        ],
        out_specs=pl.BlockSpec((PROJ_ROWS, cols), lambda i: (i, 0)),
        out_shape=jax.ShapeDtypeStruct((n, cols), F32),
        compiler_params=_params("parallel"),
        name="mixer_in_proj",
    )(x, nw, w_in)


def _bcast_last_row(v):
    return jnp.broadcast_to(v[SUBLANES - 1:SUBLANES, :], v.shape)


def _s5_build_tables(bmat_ref, ct_ref, pwr_ref, pwi_ref, mt_ref, g_ref, ht_ref):
    ts = S5_TILE_STATES
    bre, bim = bmat_ref[0, :, :ts], bmat_ref[0, :, ts:]
    cre, cim = ct_ref[0, :, :ts], ct_ref[0, :, ts:]
    cmb = jnp.concatenate([cre, -cim], axis=1).astype(BF16)
    zero_block = jnp.zeros((LANES, LANES), BF16)
    for tau in range(S5_T):
        pr, pi = pwr_ref[tau:tau + 1, :], pwi_ref[tau:tau + 1, :]
        bw = jnp.concatenate([bre * pr - bim * pi, bre * pi + bim * pr], axis=1)
        t_in = S5_T - 1 - tau
        g_ref[t_in * LANES:(t_in + 1) * LANES, :] = bw.astype(BF16)
        k_tau = lax.dot_general(bw.astype(BF16), cmb, (((1,), (1,)), ((), ())),
                                preferred_element_type=F32).astype(BF16)
        for t_in in range(S5_T - tau):
            t_out = t_in + tau
            mt_ref[t_in * LANES:(t_in + 1) * LANES, t_out * LANES:(t_out + 1) * LANES] = k_tau
        pr1, pi1 = pwr_ref[tau + 1:tau + 2, :], pwi_ref[tau + 1:tau + 2, :]
        ht_ref[tau * LANES:(tau + 1) * LANES, :] = jnp.concatenate(
            [cre * pr1 - cim * pi1, -(cre * pi1 + cim * pr1)], axis=1).astype(BF16)
    steps_per_block = S5_COL_BLOCK // LANES
    for t_out in range(S5_T):
        for t_in in range(t_out + 1, (t_out // steps_per_block + 1) * steps_per_block):
            mt_ref[t_in * LANES:(t_in + 1) * LANES, t_out * LANES:(t_out + 1) * LANES] = zero_block


def _s5_body(u_ref, bmat_ref, ct_ref, pwr_ref, pwi_ref, tre_ref, tim_ref, d_ref, o_ref,
             mt_ref, g_ref, ht_ref):
    @pl.when(pl.program_id(1) == 0)
    def _():
        _s5_build_tables(bmat_ref, ct_ref, pwr_ref, pwi_ref, mt_ref, g_ref, ht_ref)

    seq = u_ref.shape[0]
    nch = seq // S5_T
    ts = S5_TILE_STATES
    pieces = [u_ref[pl.ds(t, nch, stride=S5_T), :] for t in range(S5_T)]
    ucat = jnp.concatenate(pieces, axis=1).astype(BF16)
    x_end = jnp.dot(ucat, g_ref[...], preferred_element_type=F32)

    row_id = lax.broadcasted_iota(jnp.int32, (SUBLANES, ts), 0)
    cr = jnp.zeros((SUBLANES, ts), F32)
    ci = jnp.zeros((SUBLANES, ts), F32)
    pr = tre_ref[SCAN_CARRY]
    pi = tim_ref[SCAN_CARRY]
    prev_r, prev_i = [], []
    for k in range(nch // SUBLANES):
        rs = slice(k * SUBLANES, (k + 1) * SUBLANES)
        xr = x_end[rs, :ts]
        xi = x_end[rs, ts:]
        for s in range(LOG2_SUBLANES):
            ar = tre_ref[s]
            ai = tim_ref[s]
            sr = pltpu.roll(xr, 1 << s, axis=0)
            si = pltpu.roll(xi, 1 << s, axis=0)
            xr, xi = xr + ar * sr - ai * si, xi + ar * si + ai * sr
        xr, xi = xr + pr * cr - pi * ci, xi + pr * ci + pi * cr
        prev_r.append(jnp.where(row_id == 0, cr, pltpu.roll(xr, 1, axis=0)))
        prev_i.append(jnp.where(row_id == 0, ci, pltpu.roll(xi, 1, axis=0)))
        cr = _bcast_last_row(xr)
        ci = _bcast_last_row(xi)
    xprev = jnp.concatenate(
        [jnp.concatenate(prev_r, axis=0), jnp.concatenate(prev_i, axis=0)], axis=1).astype(BF16)
    y_carry = lax.dot_general(xprev, ht_ref[...], (((1,), (1,)), ((), ())),
                              preferred_element_type=F32)

    steps_per_block = S5_COL_BLOCK // LANES
    for c in range(S5_T // steps_per_block):
        cols = slice(c * S5_COL_BLOCK, (c + 1) * S5_COL_BLOCK)
        k_hi = (c + 1) * S5_COL_BLOCK
        y_c = y_carry[:, cols] + jnp.dot(ucat[:, :k_hi], mt_ref[:k_hi, cols],
                                         preferred_element_type=F32)
        for tt in range(steps_per_block):
            t = c * steps_per_block + tt
            y_t = y_c[:, tt * LANES:(tt + 1) * LANES] + d_ref[...] * pieces[t]
            o_ref[pl.ds(t, nch, stride=S5_T), :] = jax.nn.gelu(y_t)


def _s5(proj, col_block, batch, seq, tabs, d_skip):
    n = proj.shape[0]
    n_tiles = BRANCH_WIDTH // LANES
    tl = S5_T * LANES
    ts = S5_TILE_STATES
    return pl.pallas_call(
        _s5_body,
        grid=(n_tiles, batch),
        in_specs=[
            pl.BlockSpec((seq, LANES), lambda j, b: (b, col_block * n_tiles + j)),
            pl.BlockSpec((1, LANES, 2 * ts), lambda j, b: (j, 0, 0)),
            pl.BlockSpec((1, LANES, 2 * ts), lambda j, b: (j, 0, 0)),
            pl.BlockSpec((S5_T + 1, ts), lambda j, b: (0, j)),
            pl.BlockSpec((S5_T + 1, ts), lambda j, b: (0, j)),
            pl.BlockSpec((SCAN_TABLES, SUBLANES, ts), lambda j, b: (0, 0, j)),
            pl.BlockSpec((SCAN_TABLES, SUBLANES, ts), lambda j, b: (0, 0, j)),
            pl.BlockSpec((1, LANES), lambda j, b: (0, j)),
        ],
        out_specs=pl.BlockSpec((seq, LANES), lambda j, b: (b, j)),
        out_shape=jax.ShapeDtypeStruct((n, BRANCH_WIDTH), F32),
        scratch_shapes=[
            pltpu.VMEM((tl, tl), BF16),
            pltpu.VMEM((tl, 2 * ts), BF16),
            pltpu.VMEM((tl, 2 * ts), BF16),
        ],
        compiler_params=_params("parallel", "arbitrary"),
        name="s5_mixer",
    )(proj, tabs["bmat"], tabs["ct"], tabs["pwr"], tabs["pwi"], tabs["tre"], tabs["tim"], d_skip)


def _rglru_body(x_ref, g_ref, cw_ref, cb_ref, wa_ref, ba_ref, wx_ref, bx_ref, c1_ref, o_ref,
                xbuf_ref, carry_ref):
    rows = x_ref.shape[0]
    first = pl.program_id(1) == 0

    @pl.when(first)
    def _():
        xbuf_ref[0:SUBLANES, :] = jnp.zeros((SUBLANES, x_ref.shape[1]), F32)
        carry_ref[...] = jnp.zeros(carry_ref.shape, F32)

    @pl.when(jnp.logical_not(first))
    def _():
        xbuf_ref[0:SUBLANES, :] = xbuf_ref[rows:rows + SUBLANES, :]

    xbuf_ref[SUBLANES:SUBLANES + rows, :] = x_ref[...]
    xc = cb_ref[...]
    for k in range(CONV_WIDTH):
        off = SUBLANES - (CONV_WIDTH - 1) + k
        xc = xc + cw_ref[k:k + 1, :] * xbuf_ref[off:off + rows, :]
    xcb = xc.astype(BF16)
    r = jax.nn.sigmoid(jnp.dot(xcb, wa_ref[...], preferred_element_type=F32) + ba_ref[...])
    i = jax.nn.sigmoid(jnp.dot(xcb, wx_ref[...], preferred_element_type=F32) + bx_ref[...])
    log_a = c1_ref[...] * r
    a_all = jnp.exp(log_a)
    b_all = jnp.sqrt(1.0 - a_all * a_all) * (i * xc)
    row_id = lax.broadcasted_iota(jnp.int32, (SUBLANES, x_ref.shape[1]), 0)
    carry = carry_ref[...]
    h_blocks = []
    for k in range(rows // SUBLANES):
        rs = slice(k * SUBLANES, (k + 1) * SUBLANES)
        a = a_all[rs]
        b = b_all[rs]
        for s in range(LOG2_SUBLANES):
            d = 1 << s
            m = row_id >= d
            sa = jnp.where(m, pltpu.roll(a, d, axis=0), 1.0)
            sb = jnp.where(m, pltpu.roll(b, d, axis=0), 0.0)
            a, b = a * sa, a * sb + b
        h = a * carry + b
        carry = _bcast_last_row(h)
        h_blocks.append(h)
    carry_ref[...] = carry
    o_ref[...] = jnp.concatenate(h_blocks, axis=0) * jax.nn.gelu(g_ref[...])


def _rglru(proj, x_block, gate_block, batch, seq, conv_w, conv_b, wa, ba, wx, bx, c1):
    n = proj.shape[0]
    bw = BRANCH_WIDTH
    steps = seq // SEQ_ROWS
    row_spec = lambda cb: pl.BlockSpec((SEQ_ROWS, bw), lambda b, l: (b * steps + l, cb))
    return pl.pallas_call(
        _rglru_body,
        grid=(batch, steps),
        in_specs=[
            row_spec(x_block), row_spec(gate_block),
            _const_spec((CONV_WIDTH, bw)), _const_spec((1, bw)),
            _const_spec((bw, bw)), _const_spec((1, bw)),
            _const_spec((bw, bw)), _const_spec((1, bw)),
            _const_spec((1, bw)),
        ],
        out_specs=pl.BlockSpec((SEQ_ROWS, bw), lambda b, l: (b * steps + l, 0)),
        out_shape=jax.ShapeDtypeStruct((n, bw), F32),
        scratch_shapes=[
            pltpu.VMEM((SEQ_ROWS + SUBLANES, bw), F32),
            pltpu.VMEM((SUBLANES, bw), F32),
        ],
        compiler_params=_params("parallel", "arbitrary"),
        name="rglru_mixer",
    )(proj, proj, conv_w, conv_b, wa, ba, wx, bx, c1)


def _rows_bcast(v, row):
    return jnp.broadcast_to(v[row:row + 1, :], (SUBLANES, v.shape[1]))


def _boundary_rows(b, m):
    c, w = b.shape
    if 2 * m >= SUBLANES:
        pieces = []
        for base in range(0, c, 2 * m):
            r = base + m - 1
            pieces.append(jnp.broadcast_to(b[r:r + 1, :], (2 * m, w)))
        return jnp.concatenate(pieces, axis=0)
    sub = lax.broadcasted_iota(jnp.int32, (SUBLANES, w), 0)
    pieces = []
    for base in range(0, c, SUBLANES):
        blk = b[base:base + SUBLANES, :]
        if m == 2:
            out = jnp.where(sub < 4, _rows_bcast(blk, 1), _rows_bcast(blk, 5))
        else:
            out = jnp.where(sub < 2, _rows_bcast(blk, 0),
                            jnp.where(sub < 4, _rows_bcast(blk, 2),
                                      jnp.where(sub < 6, _rows_bcast(blk, 4), _rows_bcast(blk, 6))))
        pieces.append(out)
    return jnp.concatenate(pieces, axis=0)


def _chunk_cumsum(x):
    sub = lax.broadcasted_iota(jnp.int32, (SUBLANES, x.shape[1]), 0)
    blocks = []
    carry = None
    for r in range(0, x.shape[0], SUBLANES):
        v = x[r:r + SUBLANES, :]
        for s in range(LOG2_SUBLANES):
            d = 1 << s
            v = v + jnp.where(sub >= d, pltpu.roll(v, d, axis=0), 0.0)
        if carry is not None:
            v = v + carry
        carry = _bcast_last_row(v)
        blocks.append(v)
    return jnp.concatenate(blocks, axis=0)


def _hgrn2_body(q_ref, f_ref, v_ref, g_ref, lb_ref, nw_ref, o_ref, st_ref):
    @pl.when(pl.program_id(1) == 0)
    def _():
        st_ref[...] = jnp.zeros(st_ref.shape, F32)

    rows = q_ref.shape[0]
    c = HG_CHUNK
    t_id = lax.broadcasted_iota(jnp.int32, (c, c), 0)
    s_id = lax.broadcasted_iota(jnp.int32, (c, c), 1)
    row_id = lax.broadcasted_iota(jnp.int32, (c, HG_DK), 0)
    levels = [1 << e for e in range(int(math.log2(c)))]
    pair_mask = {m: ((t_id >> (e + 1)) == (s_id >> (e + 1))) & ((t_id & m) != 0) & ((s_id & m) == 0)
                 for e, m in enumerate(levels)}
    right_half = {m: (row_id & m) != 0 for m in levels}

    for hd in range(HG_HEADS):
        lanes = slice(hd * HG_DK, (hd + 1) * HG_DK)
        lb = lb_ref[:, lanes]
        st = st_ref[hd]
        for ck in range(rows // c):
            rs = slice(ck * c, (ck + 1) * c)
            qraw = q_ref[rs, lanes]
            z = f_ref[rs, lanes]
            v = v_ref[rs, lanes]
            q = qraw * jax.nn.sigmoid(qraw)
            sig = jax.nn.sigmoid(z)
            f = lb + (1.0 - lb) * sig
            logf = jnp.log2(f)
            k = (1.0 - lb) * (1.0 - sig)
            b = _chunk_cumsum(logf)
            vb = v.astype(BF16)
            scores = jnp.zeros((c, c), F32)
            for m in levels:
                rh = right_half[m]
                if m == 1:
                    w = jnp.where(rh, f, 1.0)
                else:
                    d = b - _boundary_rows(b, m)
                    w = jnp.exp2(jnp.minimum(d, -d))
                xm = (jnp.where(rh, q, k) * w).astype(BF16)
                part = lax.dot_general(xm, xm, (((1,), (1,)), ((), ())), preferred_element_type=F32)
                scores = jnp.where(pair_mask[m], part, scores)
            o = jnp.dot(scores.astype(BF16), vb, preferred_element_type=F32)
            o = o + jnp.sum(q * k, axis=-1, keepdims=True) * v
            qd = (q * jnp.exp2(b)).astype(BF16)
            o = o + lax.dot_general(qd, st.astype(BF16), (((1,), (1,)), ((), ())),
                                    preferred_element_type=F32)
            b_last = b[c - 1:c, :]
            kd = (k * jnp.exp2(b_last - b)).astype(BF16)
            st = st * jnp.exp2(b_last) + lax.dot_general(
                vb, kd, (((0,), (0,)), ((), ())), preferred_element_type=F32)
            o = o * lax.rsqrt(jnp.mean(o * o, axis=-1, keepdims=True) + EPS) * nw_ref[:, lanes]
            graw = g_ref[rs, lanes]
            o_ref[rs, lanes] = o * (graw * jax.nn.sigmoid(graw))
        st_ref[hd] = st


def _hgrn2(proj, q_block, f_block, v_block, g_block, batch, seq, lb, norm_w):
    n = proj.shape[0]
    bw = BRANCH_WIDTH
    steps = seq // HG_ROWS
    row_spec = lambda cb: pl.BlockSpec((HG_ROWS, bw), lambda b, l: (b * steps + l, cb))
    return pl.pallas_call(
        _hgrn2_body,
        grid=(batch, steps),
        in_specs=[row_spec(q_block), row_spec(f_block), row_spec(v_block), row_spec(g_block),
                  _const_spec((1, bw)), _const_spec((1, bw))],
        out_specs=pl.BlockSpec((HG_ROWS, bw), lambda b, l: (b * steps + l, 0)),
        out_shape=jax.ShapeDtypeStruct((n, bw), F32),
        scratch_shapes=[pltpu.VMEM((HG_HEADS, HG_DV, HG_DK), F32)],
        compiler_params=_params("parallel", "arbitrary"),
        name="hgrn2_mixer",
    )(proj, proj, proj, proj, lb, norm_w)


def _merge_body(x_ref, nw_ref, za_ref, yb_ref, yc_ref, *rest):
    wg_refs, (gw_ref, gb_ref, p_ref, wo_ref, o_ref) = rest[:-5], rest[-5:]
    per_gate = len(wg_refs) // N_BRANCH
    half = x_ref.shape[0] // 2
    wgs = [w[...].astype(BF16) for w in wg_refs]
    for r in range(2):
        rs = slice(r * half, (r + 1) * half)
        x = x_ref[rs, :]
        h = _rmsnorm(x, nw_ref[...]).astype(BF16)
        za = za_ref[rs, :]
        glu = jnp.dot(za.astype(BF16), gw_ref[...], preferred_element_type=F32) + gb_ref[...]
        ya = za * jax.nn.sigmoid(glu)
        merged = None
        for n, y in enumerate((ya, yb_ref[rs, :], yc_ref[rs, :])):
            up = jnp.dot(y.astype(BF16), p_ref[n], preferred_element_type=F32)
            gate = jnp.concatenate(
                [jnp.dot(h, w, preferred_element_type=F32)
                 for w in wgs[n * per_gate:(n + 1) * per_gate]], axis=1)
            term = jax.nn.sigmoid(gate) * up
            merged = term if merged is None else merged + term
        o_ref[rs, :] = x + jnp.dot(merged.astype(BF16), wo_ref[...], preferred_element_type=F32)


def _merge(x, nw, za, yb, yc, w_in, first_gate_block, glu_w, glu_b, p, wo, layer):
    n, d = x.shape
    bw = BRANCH_WIDTH
    n_gate_blocks = N_BRANCH * d // bw
    y_spec = pl.BlockSpec((MERGE_ROWS, bw), lambda i: (i, 0))
    gate_specs = [_stacked_spec(w_in, (layer,), block_tail=(d, bw), tail_index=(0, first_gate_block + k))
                  for k in range(n_gate_blocks)]
    return pl.pallas_call(
        _merge_body,
        grid=(n // MERGE_ROWS,),
        in_specs=[pl.BlockSpec((MERGE_ROWS, d), lambda i: (i, 0)), _const_spec((1, d)),
                  y_spec, y_spec, y_spec] + gate_specs
        + [_stacked_spec(glu_w, (layer,)), _const_spec((1, bw)),
           _stacked_spec(p, (layer,)), _stacked_spec(wo, (layer,))],
        out_specs=pl.BlockSpec((MERGE_ROWS, d), lambda i: (i, 0)),
        out_shape=jax.ShapeDtypeStruct((n, d), F32),
        compiler_params=_params("parallel"),
        name="merge_out_proj",
    )(x, nw, za, yb, yc, *([w_in] * n_gate_blocks), glu_w, glu_b, p, wo)


def _s5_tables(lam_re, lam_im, log_dt, b_re, b_im, c_re, c_im):
    groups, states = lam_re.shape
    n_tiles = groups // S5_GROUPS_PER_TILE
    lr = jnp.minimum(lam_re.astype(F32), S5_EIG_MAX)
    li = lam_im.astype(F32)
    dt = jnp.exp(log_dt.astype(F32))[:, None]
    mag = jnp.exp(lr * dt)
    ar = mag * jnp.cos(li * dt)
    ai = mag * jnp.sin(li * dt)
    den = lr * lr + li * li
    fr = ((ar - 1.0) * lr + ai * li) / den
    fi = (ai * lr - (ar - 1.0) * li) / den
    br, bi = b_re.astype(F32), b_im.astype(F32)
    bbr = fr[..., None] * br - fi[..., None] * bi
    bbi = fr[..., None] * bi + fi[..., None] * br
    gpt = S5_GROUPS_PER_TILE
    eye = jnp.eye(gpt, dtype=F32)

    def in_tiles(m):
        m = m.reshape(n_tiles, gpt, states, S5_GROUP)
        return jnp.einsum('jgpc,gh->jgchp', m, eye).reshape(n_tiles, LANES, S5_TILE_STATES)

    def out_tiles(m):
        m = m.reshape(n_tiles, gpt, S5_GROUP, states)
        return jnp.einsum('jgcp,gh->jgchp', m, eye).reshape(n_tiles, LANES, S5_TILE_STATES)

    bmat = jnp.concatenate([in_tiles(bbr), in_tiles(bbi)], axis=2)
    ct = jnp.concatenate([out_tiles(c_re.astype(F32)), out_tiles(c_im.astype(F32))], axis=2)

    def power(kk):
        m = jnp.exp(lr * dt * kk)
        return m * jnp.cos(li * dt * kk), m * jnp.sin(li * dt * kk)

    pwr, pwi = power(jnp.arange(S5_T + 1, dtype=F32)[:, None, None])
    pwr = pwr.reshape(S5_T + 1, groups * states)
    pwi = pwi.reshape(S5_T + 1, groups * states)

    rows = jnp.arange(SUBLANES, dtype=F32)
    tre, tim = [], []
    for s in range(LOG2_SUBLANES):
        d = float(1 << s)
        sr, si = power(jnp.full((SUBLANES, 1, 1), d * S5_T, F32))
        keep = (rows >= d)[:, None, None]
        tre.append(jnp.where(keep, sr, 0.0))
        tim.append(jnp.where(keep, si, 0.0))
    sr, si = power((rows[:, None, None] + 1.0) * S5_T)
    tre.append(sr)
    tim.append(si)
    tre = jnp.stack(tre).reshape(SCAN_TABLES, SUBLANES, groups * states)
    tim = jnp.stack(tim).reshape(SCAN_TABLES, SUBLANES, groups * states)
    return {"bmat": bmat, "ct": ct, "pwr": pwr, "pwi": pwi, "tre": tre, "tim": tim}


def _block_diag(w):
    nb, bi, bo = w.shape
    eye = jnp.eye(nb, dtype=w.dtype)
    return jnp.einsum('hij,hk->hikj', w, eye).reshape(nb * bi, nb * bo)


def kernel(x, norm_w, final_norm_w, ffn_gate, ffn_up, ffn_down, w_in, branch_proj, w_out, s5_lambda_re, s5_lambda_im, s5_log_dt, s5_b_re, s5_b_im, s5_c_re, s5_c_im, s5_d, s5_glu_w, s5_glu_b, hg_lb_logits, hg_norm_w, rg_conv_w, rg_conv_b, rg_wa, rg_ba, rg_wx, rg_bx, rg_lambda):
    batch, seq, d = x.shape
    depth = norm_w.shape[0]
    col = {name: k for k, name in enumerate(("u", "q", "f", "v", "g", "xc", "gc"))}
    mix_cols = len(col) * BRANCH_WIDTH

    p = jax.nn.softmax(hg_lb_logits.astype(F32), axis=0)
    lower_bounds = jnp.cumsum(p, axis=0) - p[0]

    ffn_w = (ffn_gate, ffn_up, ffn_down)
    w_in_b = w_in
    glu_w_b, branch_proj_b, w_out_b = (s5_glu_w.astype(BF16), branch_proj.astype(BF16),
                                       w_out.astype(BF16))

    h = x.reshape(batch * seq, d)
    row = lambda v: v.reshape(1, -1).astype(F32)
    for l in range(depth):
        h = _ffn(h, row(norm_w[l, 0]), *ffn_w, (l, 0))
        proj = _proj(h, row(norm_w[l, 1]), w_in_b, l, mix_cols)
        tabs = _s5_tables(s5_lambda_re[l], s5_lambda_im[l], s5_log_dt[l], s5_b_re[l], s5_b_im[l],
                          s5_c_re[l], s5_c_im[l])
        za = _s5(proj, col["u"], batch, seq, tabs, row(s5_d[l]))
        yb = _hgrn2(proj, col["q"], col["f"], col["v"], col["g"], batch, seq,
                    row(lower_bounds[l]), row(hg_norm_w[l]))
        c1 = -RG_C * jax.nn.softplus(-rg_lambda[l].astype(F32))
        yc = _rglru(proj, col["xc"], col["gc"], batch, seq, rg_conv_w[l].astype(F32),
                    row(rg_conv_b[l]), _block_diag(rg_wa[l]).astype(BF16), row(rg_ba[l]),
                    _block_diag(rg_wx[l]).astype(BF16), row(rg_bx[l]), row(c1))
        h = _merge(h, row(norm_w[l, 1]), za, yb, yc, w_in_b, len(col), glu_w_b,
                   row(s5_glu_b[l]), branch_proj_b, w_out_b, l)
        last = l == depth - 1
        h = _ffn(h, row(norm_w[l, 2]), *ffn_w, (l, 1),
                 final_w=row(final_norm_w) if last else None)
    return h.reshape(batch, seq, d)
```

```python
import functools
import math

import jax
import jax.numpy as jnp
from jax import lax
from jax.experimental import pallas as pl
from jax.experimental.pallas import tpu as pltpu

F32 = jnp.float32
BF16 = jnp.bfloat16

LANES = 128
SUBLANES = 8
LOG2_SUBLANES = 3
MXU_WIDTH = 256
VMEM_LIMIT_BYTES = 56 * 1024 * 1024

SCAN_CARRY = LOG2_SUBLANES
SCAN_TABLES = LOG2_SUBLANES + 1

EPS = 1e-6
BRANCH_WIDTH = 512
N_BRANCH = 3
S5_GROUP = 16
S5_STATE = 64
S5_EIG_MAX = -1e-4
S5_GROUPS_PER_TILE = LANES // S5_GROUP
S5_TILE_STATES = S5_GROUPS_PER_TILE * S5_STATE
S5_T = 8
S5_COL_BLOCK = MXU_WIDTH
HG_HEADS = 4
HG_DK = 128
HG_DV = 128
HG_CHUNK = 128
RG_BLOCKS = 8
RG_BLOCK = 64
RG_C = 8.0
CONV_WIDTH = 4

FFN_ROWS = 512
FFN_FCHUNK = 256
PROJ_ROWS = 512
SEQ_ROWS = 1024
HG_ROWS = 1024
MERGE_ROWS = 512


def _rmsnorm(x, w):
    ms = jnp.mean(x * x, axis=-1, keepdims=True)
    return x * lax.rsqrt(ms + EPS) * w


def _gelu(x):
    c0 = math.sqrt(2.0 / math.pi)
    hx = 0.5 * x
    return hx + hx * jnp.tanh(x * (c0 + (0.044715 * c0) * (x * x)))


def _const_spec(shape, single_buffer=False):
    nd = len(shape)
    kw = {"pipeline_mode": pl.Buffered(1)} if single_buffer else {}
    return pl.BlockSpec(shape, lambda *_: (0,) * nd, **kw)


def _stacked_spec(arr, lead, block_tail=None, tail_index=None):
    tail = tuple(arr.shape[len(lead):]) if block_tail is None else tuple(block_tail)
    idx = tuple(lead) + (tuple(tail_index) if tail_index is not None else (0,) * len(tail))
    return pl.BlockSpec((None,) * len(lead) + tail, lambda *_: idx, pipeline_mode=pl.Buffered(1))


def _params(*sem):
    return pltpu.CompilerParams(dimension_semantics=sem, vmem_limit_bytes=VMEM_LIMIT_BYTES)


def _ffn_body(x_ref, nw_ref, wg_ref, wu_ref, wd_ref, *rest, final):
    o_ref = rest[-1]
    x = x_ref[...]
    h = _rmsnorm(x, nw_ref[...]).astype(BF16)
    acc = jnp.zeros(x.shape, F32)
    for c in range(wg_ref.shape[1] // FFN_FCHUNK):
        cols = slice(c * FFN_FCHUNK, (c + 1) * FFN_FCHUNK)
        g = jnp.dot(h, wg_ref[:, cols].astype(BF16), preferred_element_type=F32)
        u = jnp.dot(h, wu_ref[:, cols].astype(BF16), preferred_element_type=F32)
        a = (g * jax.nn.sigmoid(g) * u).astype(BF16)
        acc = acc + jnp.dot(a, wd_ref[cols, :].astype(BF16), preferred_element_type=F32)
    y = x + 0.5 * acc
    if final:
        y = _rmsnorm(y, rest[0][...])
    o_ref[...] = y


def _ffn(x, nw, wg, wu, wd, which, final_w=None):
    n, d = x.shape
    final = final_w is not None
    in_specs = [
        pl.BlockSpec((FFN_ROWS, d), lambda i: (i, 0)),
        _const_spec((1, d)),
        _stacked_spec(wg, which),
        _stacked_spec(wu, which),
        _stacked_spec(wd, which),
    ]
    args = [x, nw, wg, wu, wd]
    if final:
        in_specs.append(_const_spec((1, d)))
        args.append(final_w)
    return pl.pallas_call(
        functools.partial(_ffn_body, final=final),
        grid=(n // FFN_ROWS,),
        in_specs=in_specs,
        out_specs=pl.BlockSpec((FFN_ROWS, d), lambda i: (i, 0)),
        out_shape=jax.ShapeDtypeStruct((n, d), F32),
        compiler_params=_params("parallel"),
        name="ffn_final" if final else "ffn",
    )(*args)


def _proj_body(x_ref, nw_ref, w_ref, o_ref):
    h = _rmsnorm(x_ref[...], nw_ref[...]).astype(BF16)
    o_ref[...] = jnp.dot(h, w_ref[...].astype(BF16), preferred_element_type=F32)


def _proj(x, nw, w_in, layer, cols):
    n, d = x.shape
    return pl.pallas_call(
        _proj_body,
        grid=(n // PROJ_ROWS,),
        in_specs=[
            pl.BlockSpec((PROJ_ROWS, d), lambda i: (i, 0)),
            _const_spec((1, d)),
            _stacked_spec(w_in, (layer,), block_tail=(d, cols), tail_index=(0, 0)),
        ],
        out_specs=pl.BlockSpec((PROJ_ROWS, cols), lambda i: (i, 0)),
        out_shape=jax.ShapeDtypeStruct((n, cols), F32),
        compiler_params=_params("parallel"),
        name="mixer_in_proj",
    )(x, nw, w_in)


def _bcast_last_row(v):
    return jnp.broadcast_to(v[SUBLANES - 1:SUBLANES, :], v.shape)


def _s5_build_tables(bmat_ref, ct_ref, pwr_ref, pwi_ref, mt_ref, g_ref, ht_ref):
    ts = S5_TILE_STATES
    bre, bim = bmat_ref[0, :, :ts], bmat_ref[0, :, ts:]
    cre, cim = ct_ref[0, :, :ts], ct_ref[0, :, ts:]
    cmb = jnp.concatenate([cre, -cim], axis=1).astype(BF16)
    zero_block = jnp.zeros((LANES, LANES), BF16)
    for tau in range(S5_T):
        pr, pi = pwr_ref[tau:tau + 1, :], pwi_ref[tau:tau + 1, :]
        bw = jnp.concatenate([bre * pr - bim * pi, bre * pi + bim * pr], axis=1)
        t_in = S5_T - 1 - tau
        g_ref[t_in * LANES:(t_in + 1) * LANES, :] = bw.astype(BF16)
        k_tau = lax.dot_general(bw.astype(BF16), cmb, (((1,), (1,)), ((), ())),
                                preferred_element_type=F32).astype(BF16)
        for t_in in range(S5_T - tau):
            t_out = t_in + tau
            mt_ref[t_in * LANES:(t_in + 1) * LANES, t_out * LANES:(t_out + 1) * LANES] = k_tau
        pr1, pi1 = pwr_ref[tau + 1:tau + 2, :], pwi_ref[tau + 1:tau + 2, :]
        ht_ref[tau * LANES:(tau + 1) * LANES, :] = jnp.concatenate(
            [cre * pr1 - cim * pi1, -(cre * pi1 + cim * pr1)], axis=1).astype(BF16)
    steps_per_block = S5_COL_BLOCK // LANES
    for t_out in range(S5_T):
        for t_in in range(t_out + 1, (t_out // steps_per_block + 1) * steps_per_block):
            mt_ref[t_in * LANES:(t_in + 1) * LANES, t_out * LANES:(t_out + 1) * LANES] = zero_block


def _s5_body(u_ref, bmat_ref, ct_ref, pwr_ref, pwi_ref, tre_ref, tim_ref, d_ref, o_ref,
             mt_ref, g_ref, ht_ref):
    @pl.when(pl.program_id(1) == 0)
    def _():
        _s5_build_tables(bmat_ref, ct_ref, pwr_ref, pwi_ref, mt_ref, g_ref, ht_ref)

    seq = u_ref.shape[0]
    nch = seq // S5_T
    ts = S5_TILE_STATES
    pieces = [u_ref[pl.ds(t, nch, stride=S5_T), :] for t in range(S5_T)]
    ucat = jnp.concatenate(pieces, axis=1).astype(BF16)
    x_end = jnp.dot(ucat, g_ref[...], preferred_element_type=F32)

    row_id = lax.broadcasted_iota(jnp.int32, (SUBLANES, ts), 0)
    cr = jnp.zeros((SUBLANES, ts), F32)
    ci = jnp.zeros((SUBLANES, ts), F32)
    pr = tre_ref[SCAN_CARRY]
    pi = tim_ref[SCAN_CARRY]
    prev_r, prev_i = [], []
    for k in range(nch // SUBLANES):
        rs = slice(k * SUBLANES, (k + 1) * SUBLANES)
        xr = x_end[rs, :ts]
        xi = x_end[rs, ts:]
        for s in range(LOG2_SUBLANES):
            ar = tre_ref[s]
            ai = tim_ref[s]
            sr = pltpu.roll(xr, 1 << s, axis=0)
            si = pltpu.roll(xi, 1 << s, axis=0)
            xr, xi = xr + ar * sr - ai * si, xi + ar * si + ai * sr
        xr, xi = xr + pr * cr - pi * ci, xi + pr * ci + pi * cr
        prev_r.append(jnp.where(row_id == 0, cr, pltpu.roll(xr, 1, axis=0)))
        prev_i.append(jnp.where(row_id == 0, ci, pltpu.roll(xi, 1, axis=0)))
        cr = _bcast_last_row(xr)
        ci = _bcast_last_row(xi)
    xprev = jnp.concatenate(
        [jnp.concatenate(prev_r, axis=0), jnp.concatenate(prev_i, axis=0)], axis=1).astype(BF16)
    y_carry = lax.dot_general(xprev, ht_ref[...], (((1,), (1,)), ((), ())),
                              preferred_element_type=F32)

    steps_per_block = S5_COL_BLOCK // LANES
    for c in range(S5_T // steps_per_block):
        cols = slice(c * S5_COL_BLOCK, (c + 1) * S5_COL_BLOCK)
        k_hi = (c + 1) * S5_COL_BLOCK
        y_c = y_carry[:, cols] + jnp.dot(ucat[:, :k_hi], mt_ref[:k_hi, cols],
                                         preferred_element_type=F32)
        for tt in range(steps_per_block):
            t = c * steps_per_block + tt
            y_t = y_c[:, tt * LANES:(tt + 1) * LANES] + d_ref[...] * pieces[t]
            o_ref[pl.ds(t, nch, stride=S5_T), :] = _gelu(y_t)


def _s5(proj, col_block, batch, seq, tabs, d_skip):
    n = proj.shape[0]
    n_tiles = BRANCH_WIDTH // LANES
    tl = S5_T * LANES
    ts = S5_TILE_STATES
    return pl.pallas_call(
        _s5_body,
        grid=(n_tiles, batch),
        in_specs=[
            pl.BlockSpec((seq, LANES), lambda j, b: (b, col_block * n_tiles + j)),
            pl.BlockSpec((1, LANES, 2 * ts), lambda j, b: (j, 0, 0)),
            pl.BlockSpec((1, LANES, 2 * ts), lambda j, b: (j, 0, 0)),
            pl.BlockSpec((S5_T + 1, ts), lambda j, b: (0, j)),
            pl.BlockSpec((S5_T + 1, ts), lambda j, b: (0, j)),
            pl.BlockSpec((SCAN_TABLES, SUBLANES, ts), lambda j, b: (0, 0, j)),
            pl.BlockSpec((SCAN_TABLES, SUBLANES, ts), lambda j, b: (0, 0, j)),
            pl.BlockSpec((1, LANES), lambda j, b: (0, j)),
        ],
        out_specs=pl.BlockSpec((seq, LANES), lambda j, b: (b, j)),
        out_shape=jax.ShapeDtypeStruct((n, BRANCH_WIDTH), F32),
        scratch_shapes=[
            pltpu.VMEM((tl, tl), BF16),
            pltpu.VMEM((tl, 2 * ts), BF16),
            pltpu.VMEM((tl, 2 * ts), BF16),
        ],
        compiler_params=_params("parallel", "arbitrary"),
        name="s5_mixer",
    )(proj, tabs["bmat"], tabs["ct"], tabs["pwr"], tabs["pwi"], tabs["tre"], tabs["tim"], d_skip)


def _rglru_body(x_ref, g_ref, cw_ref, cb_ref, wa_ref, ba_ref, wx_ref, bx_ref, c1_ref, o_ref,
                xbuf_ref, carry_ref):
    rows = x_ref.shape[0]
    first = pl.program_id(1) == 0

    @pl.when(first)
    def _():
        xbuf_ref[0:SUBLANES, :] = jnp.zeros((SUBLANES, x_ref.shape[1]), F32)
        carry_ref[...] = jnp.zeros(carry_ref.shape, F32)

    @pl.when(jnp.logical_not(first))
    def _():
        xbuf_ref[0:SUBLANES, :] = xbuf_ref[rows:rows + SUBLANES, :]

    xbuf_ref[SUBLANES:SUBLANES + rows, :] = x_ref[...]
    xc = cb_ref[...]
    for k in range(CONV_WIDTH):
        off = SUBLANES - (CONV_WIDTH - 1) + k
        xc = xc + cw_ref[k:k + 1, :] * xbuf_ref[off:off + rows, :]
    xcb = xc.astype(BF16)
    r = jax.nn.sigmoid(jnp.dot(xcb, wa_ref[...], preferred_element_type=F32) + ba_ref[...])
    i = jax.nn.sigmoid(jnp.dot(xcb, wx_ref[...], preferred_element_type=F32) + bx_ref[...])
    log_a = c1_ref[...] * r
    a_all = jnp.exp(log_a)
    b_all = jnp.sqrt(1.0 - a_all * a_all) * (i * xc)
    row_id = lax.broadcasted_iota(jnp.int32, (SUBLANES, x_ref.shape[1]), 0)
    carry = carry_ref[...]
    h_blocks = []
    for k in range(rows // SUBLANES):
        rs = slice(k * SUBLANES, (k + 1) * SUBLANES)
        a = a_all[rs]
        b = b_all[rs]
        for s in range(LOG2_SUBLANES):
            d = 1 << s
            m = row_id >= d
            sa = jnp.where(m, pltpu.roll(a, d, axis=0), 1.0)
            sb = jnp.where(m, pltpu.roll(b, d, axis=0), 0.0)
            a, b = a * sa, a * sb + b
        h = a * carry + b
        carry = _bcast_last_row(h)
        h_blocks.append(h)
    carry_ref[...] = carry
    o_ref[...] = jnp.concatenate(h_blocks, axis=0) * _gelu(g_ref[...])


def _rglru(proj, x_block, gate_block, batch, seq, conv_w, conv_b, wa, ba, wx, bx, c1):
    n = proj.shape[0]
    bw = BRANCH_WIDTH
    steps = seq // SEQ_ROWS
    row_spec = lambda cb: pl.BlockSpec((SEQ_ROWS, bw), lambda b, l: (b * steps + l, cb))
    return pl.pallas_call(
        _rglru_body,
        grid=(batch, steps),
        in_specs=[
            row_spec(x_block), row_spec(gate_block),
            _const_spec((CONV_WIDTH, bw)), _const_spec((1, bw)),
            _const_spec((bw, bw)), _const_spec((1, bw)),
            _const_spec((bw, bw)), _const_spec((1, bw)),
            _const_spec((1, bw)),
        ],
        out_specs=pl.BlockSpec((SEQ_ROWS, bw), lambda b, l: (b * steps + l, 0)),
        out_shape=jax.ShapeDtypeStruct((n, bw), F32),
        scratch_shapes=[
            pltpu.VMEM((SEQ_ROWS + SUBLANES, bw), F32),
            pltpu.VMEM((SUBLANES, bw), F32),
        ],
        compiler_params=_params("parallel", "arbitrary"),
        name="rglru_mixer",
    )(proj, proj, conv_w, conv_b, wa, ba, wx, bx, c1)


def _rows_bcast(v, row):
    return jnp.broadcast_to(v[row:row + 1, :], (SUBLANES, v.shape[1]))


def _boundary_rows(b, m):
    c, w = b.shape
    if 2 * m >= SUBLANES:
        pieces = []
        for base in range(0, c, 2 * m):
            r = base + m - 1
            pieces.append(jnp.broadcast_to(b[r:r + 1, :], (2 * m, w)))
        return jnp.concatenate(pieces, axis=0)
    sub = lax.broadcasted_iota(jnp.int32, (SUBLANES, w), 0)
    pieces = []
    for base in range(0, c, SUBLANES):
        blk = b[base:base + SUBLANES, :]
        if m == 2:
            out = jnp.where(sub < 4, _rows_bcast(blk, 1), _rows_bcast(blk, 5))
        else:
            out = jnp.where(sub < 2, _rows_bcast(blk, 0),
                            jnp.where(sub < 4, _rows_bcast(blk, 2),
                                      jnp.where(sub < 6, _rows_bcast(blk, 4), _rows_bcast(blk, 6))))
        pieces.append(out)
    return jnp.concatenate(pieces, axis=0)


def _chunk_cumsum(x):
    sub = lax.broadcasted_iota(jnp.int32, (SUBLANES, x.shape[1]), 0)
    blocks = []
    carry = None
    for r in range(0, x.shape[0], SUBLANES):
        v = x[r:r + SUBLANES, :]
        for s in range(LOG2_SUBLANES):
            d = 1 << s
            v = v + jnp.where(sub >= d, pltpu.roll(v, d, axis=0), 0.0)
        if carry is not None:
            v = v + carry
        carry = _bcast_last_row(v)
        blocks.append(v)
    return jnp.concatenate(blocks, axis=0)


def _hgrn2_body(q_ref, f_ref, v_ref, g_ref, lb_ref, nw_ref, o_ref, st_ref):
    @pl.when(pl.program_id(1) == 0)
    def _():
        st_ref[...] = jnp.zeros(st_ref.shape, F32)

    rows = q_ref.shape[0]
    c = HG_CHUNK
    t_id = lax.broadcasted_iota(jnp.int32, (c, c), 0)
    s_id = lax.broadcasted_iota(jnp.int32, (c, c), 1)
    row_id = lax.broadcasted_iota(jnp.int32, (c, HG_DK), 0)
    levels = [1 << e for e in range(int(math.log2(c)))]
    pair_mask = {m: ((t_id >> (e + 1)) == (s_id >> (e + 1))) & ((t_id & m) != 0) & ((s_id & m) == 0)
                 for e, m in enumerate(levels)}
    right_half = {m: (row_id & m) != 0 for m in levels}

    for hd in range(HG_HEADS):
        lanes = slice(hd * HG_DK, (hd + 1) * HG_DK)
        lb = lb_ref[:, lanes]
        st = st_ref[hd]
        for ck in range(rows // c):
            rs = slice(ck * c, (ck + 1) * c)
            qraw = q_ref[rs, lanes]
            z = f_ref[rs, lanes]
            v = v_ref[rs, lanes]
            q = qraw * jax.nn.sigmoid(qraw)
            sig = jax.nn.sigmoid(z)
            f = lb + (1.0 - lb) * sig
            logf = jnp.log2(f)
            k = (1.0 - lb) * (1.0 - sig)
            b = _chunk_cumsum(logf)
            vb = v.astype(BF16)
            scores = jnp.zeros((c, c), F32)
            for m in levels:
                rh = right_half[m]
                if m == 1:
                    w = jnp.where(rh, f, 1.0)
                else:
                    d = b - _boundary_rows(b, m)
                    w = jnp.exp2(jnp.minimum(d, -d))
                xm = (jnp.where(rh, q, k) * w).astype(BF16)
                part = lax.dot_general(xm, xm, (((1,), (1,)), ((), ())), preferred_element_type=F32)
                scores = jnp.where(pair_mask[m], part, scores)
            o = jnp.dot(scores.astype(BF16), vb, preferred_element_type=F32)
            o = o + jnp.sum(q * k, axis=-1, keepdims=True) * v
            qd = (q * jnp.exp2(b)).astype(BF16)
            o = o + lax.dot_general(qd, st.astype(BF16), (((1,), (1,)), ((), ())),
                                    preferred_element_type=F32)
            b_last = b[c - 1:c, :]
            kd = (k * jnp.exp2(b_last - b)).astype(BF16)
            st = st * jnp.exp2(b_last) + lax.dot_general(
                vb, kd, (((0,), (0,)), ((), ())), preferred_element_type=F32)
            o = o * lax.rsqrt(jnp.mean(o * o, axis=-1, keepdims=True) + EPS) * nw_ref[:, lanes]
            graw = g_ref[rs, lanes]
            o_ref[rs, lanes] = o * (graw * jax.nn.sigmoid(graw))
        st_ref[hd] = st


def _hgrn2(proj, q_block, f_block, v_block, g_block, batch, seq, lb, norm_w):
    n = proj.shape[0]
    bw = BRANCH_WIDTH
    steps = seq // HG_ROWS
    row_spec = lambda cb: pl.BlockSpec((HG_ROWS, bw), lambda b, l: (b * steps + l, cb))
    return pl.pallas_call(
        _hgrn2_body,
        grid=(batch, steps),
        in_specs=[row_spec(q_block), row_spec(f_block), row_spec(v_block), row_spec(g_block),
                  _const_spec((1, bw)), _const_spec((1, bw))],
        out_specs=pl.BlockSpec((HG_ROWS, bw), lambda b, l: (b * steps + l, 0)),
        out_shape=jax.ShapeDtypeStruct((n, bw), F32),
        scratch_shapes=[pltpu.VMEM((HG_HEADS, HG_DV, HG_DK), F32)],
        compiler_params=_params("parallel", "arbitrary"),
        name="hgrn2_mixer",
    )(proj, proj, proj, proj, lb, norm_w)


def _merge_body(x_ref, nw_ref, za_ref, yb_ref, yc_ref, *rest):
    wg_refs, (gw_ref, gb_ref, p_ref, wo_ref, o_ref) = rest[:-5], rest[-5:]
    per_gate = len(wg_refs) // N_BRANCH
    half = x_ref.shape[0] // 2
    wgs = [w[...].astype(BF16) for w in wg_refs]
    for r in range(2):
        rs = slice(r * half, (r + 1) * half)
        x = x_ref[rs, :]
        h = _rmsnorm(x, nw_ref[...]).astype(BF16)
        za = za_ref[rs, :]
        glu = jnp.dot(za.astype(BF16), gw_ref[...], preferred_element_type=F32) + gb_ref[...]
        ya = za * jax.nn.sigmoid(glu)
        merged = None
        for n, y in enumerate((ya, yb_ref[rs, :], yc_ref[rs, :])):
            up = jnp.dot(y.astype(BF16), p_ref[n], preferred_element_type=F32)
            gate = jnp.concatenate(
                [jnp.dot(h, w, preferred_element_type=F32)
                 for w in wgs[n * per_gate:(n + 1) * per_gate]], axis=1)
            term = jax.nn.sigmoid(gate) * up
            merged = term if merged is None else merged + term
        o_ref[rs, :] = x + jnp.dot(merged.astype(BF16), wo_ref[...], preferred_element_type=F32)


def _merge(x, nw, za, yb, yc, w_in, first_gate_block, glu_w, glu_b, p, wo, layer):
    n, d = x.shape
    bw = BRANCH_WIDTH
    n_gate_blocks = N_BRANCH * d // bw
    y_spec = pl.BlockSpec((MERGE_ROWS, bw), lambda i: (i, 0))
    gate_specs = [_stacked_spec(w_in, (layer,), block_tail=(d, bw), tail_index=(0, first_gate_block + k))
                  for k in range(n_gate_blocks)]
    return pl.pallas_call(
        _merge_body,
        grid=(n // MERGE_ROWS,),
        in_specs=[pl.BlockSpec((MERGE_ROWS, d), lambda i: (i, 0)), _const_spec((1, d)),
                  y_spec, y_spec, y_spec] + gate_specs
        + [_stacked_spec(glu_w, (layer,)), _const_spec((1, bw)),
           _stacked_spec(p, (layer,)), _stacked_spec(wo, (layer,))],
        out_specs=pl.BlockSpec((MERGE_ROWS, d), lambda i: (i, 0)),
        out_shape=jax.ShapeDtypeStruct((n, d), F32),
        compiler_params=_params("parallel"),
        name="merge_out_proj",
    )(x, nw, za, yb, yc, *([w_in] * n_gate_blocks), glu_w, glu_b, p, wo)


def _s5_tables(lam_re, lam_im, log_dt, b_re, b_im, c_re, c_im):
    groups, states = lam_re.shape
    n_tiles = groups // S5_GROUPS_PER_TILE
    lr = jnp.minimum(lam_re.astype(F32), S5_EIG_MAX)
    li = lam_im.astype(F32)
    dt = jnp.exp(log_dt.astype(F32))[:, None]
    mag = jnp.exp(lr * dt)
    ar = mag * jnp.cos(li * dt)
    ai = mag * jnp.sin(li * dt)
    den = lr * lr + li * li
    fr = ((ar - 1.0) * lr + ai * li) / den
    fi = (ai * lr - (ar - 1.0) * li) / den
    br, bi = b_re.astype(F32), b_im.astype(F32)
    bbr = fr[..., None] * br - fi[..., None] * bi
    bbi = fr[..., None] * bi + fi[..., None] * br
    gpt = S5_GROUPS_PER_TILE
    eye = jnp.eye(gpt, dtype=F32)

    def in_tiles(m):
        m = m.reshape(n_tiles, gpt, states, S5_GROUP)
        return jnp.einsum('jgpc,gh->jgchp', m, eye).reshape(n_tiles, LANES, S5_TILE_STATES)

    def out_tiles(m):
        m = m.reshape(n_tiles, gpt, S5_GROUP, states)
        return jnp.einsum('jgcp,gh->jgchp', m, eye).reshape(n_tiles, LANES, S5_TILE_STATES)

    bmat = jnp.concatenate([in_tiles(bbr), in_tiles(bbi)], axis=2)
    ct = jnp.concatenate([out_tiles(c_re.astype(F32)), out_tiles(c_im.astype(F32))], axis=2)

    def power(kk):
        m = jnp.exp(lr * dt * kk)
        return m * jnp.cos(li * dt * kk), m * jnp.sin(li * dt * kk)

    pwr, pwi = power(jnp.arange(S5_T + 1, dtype=F32)[:, None, None])
    pwr = pwr.reshape(S5_T + 1, groups * states)
    pwi = pwi.reshape(S5_T + 1, groups * states)

    rows = jnp.arange(SUBLANES, dtype=F32)
    tre, tim = [], []
    for s in range(LOG2_SUBLANES):
        d = float(1 << s)
        sr, si = power(jnp.full((SUBLANES, 1, 1), d * S5_T, F32))
        keep = (rows >= d)[:, None, None]
        tre.append(jnp.where(keep, sr, 0.0))
        tim.append(jnp.where(keep, si, 0.0))
    sr, si = power((rows[:, None, None] + 1.0) * S5_T)
    tre.append(sr)
    tim.append(si)
    tre = jnp.stack(tre).reshape(SCAN_TABLES, SUBLANES, groups * states)
    tim = jnp.stack(tim).reshape(SCAN_TABLES, SUBLANES, groups * states)
    return {"bmat": bmat, "ct": ct, "pwr": pwr, "pwi": pwi, "tre": tre, "tim": tim}


def _block_diag(w):
    nb, bi, bo = w.shape
    eye = jnp.eye(nb, dtype=w.dtype)
    return jnp.einsum('hij,hk->hikj', w, eye).reshape(nb * bi, nb * bo)


def kernel(x, norm_w, final_norm_w, ffn_gate, ffn_up, ffn_down, w_in, branch_proj, w_out, s5_lambda_re, s5_lambda_im, s5_log_dt, s5_b_re, s5_b_im, s5_c_re, s5_c_im, s5_d, s5_glu_w, s5_glu_b, hg_lb_logits, hg_norm_w, rg_conv_w, rg_conv_b, rg_wa, rg_ba, rg_wx, rg_bx, rg_lambda):
    batch, seq, d = x.shape
    depth = norm_w.shape[0]
    col = {name: k for k, name in enumerate(("u", "q", "f", "v", "g", "xc", "gc"))}
    mix_cols = len(col) * BRANCH_WIDTH

    p = jax.nn.softmax(hg_lb_logits.astype(F32), axis=0)
    lower_bounds = jnp.cumsum(p, axis=0) - p[0]

    ffn_w = (ffn_gate, ffn_up, ffn_down)
    w_in_b = w_in
    glu_w_b, branch_proj_b, w_out_b = (s5_glu_w.astype(BF16), branch_proj.astype(BF16),
                                       w_out.astype(BF16))

    h = x.reshape(batch * seq, d)
    row = lambda v: v.reshape(1, -1).astype(F32)
    for l in range(depth):
        h = _ffn(h, row(norm_w[l, 0]), *ffn_w, (l, 0))
        proj = _proj(h, row(norm_w[l, 1]), w_in_b, l, mix_cols)
        tabs = _s5_tables(s5_lambda_re[l], s5_lambda_im[l], s5_log_dt[l], s5_b_re[l], s5_b_im[l],
                          s5_c_re[l], s5_c_im[l])
        za = _s5(proj, col["u"], batch, seq, tabs, row(s5_d[l]))
        yb = _hgrn2(proj, col["q"], col["f"], col["v"], col["g"], batch, seq,
                    row(lower_bounds[l]), row(hg_norm_w[l]))
        c1 = -RG_C * jax.nn.softplus(-rg_lambda[l].astype(F32))
        yc = _rglru(proj, col["xc"], col["gc"], batch, seq, rg_conv_w[l].astype(F32),
                    row(rg_conv_b[l]), _block_diag(rg_wa[l]).astype(BF16), row(rg_ba[l]),
                    _block_diag(rg_wx[l]).astype(BF16), row(rg_bx[l]), row(c1))
        h = _merge(h, row(norm_w[l, 1]), za, yb, yc, w_in_b, len(col), glu_w_b,
                   row(s5_glu_b[l]), branch_proj_b, w_out_b, l)
        last = l == depth - 1
        h = _ffn(h, row(norm_w[l, 2]), *ffn_w, (l, 1),
                 final_w=row(final_norm_w) if last else None)
    return h.reshape(batch, seq, d)
```

```python
import functools
import math

import jax
import jax.numpy as jnp
from jax import lax
from jax.experimental import pallas as pl
from jax.experimental.pallas import tpu as pltpu

F32 = jnp.float32
BF16 = jnp.bfloat16

LANES = 128
SUBLANES = 8
LOG2_SUBLANES = 3
MXU_WIDTH = 256
VMEM_LIMIT_BYTES = 56 * 1024 * 1024

SCAN_CARRY = LOG2_SUBLANES
SCAN_TABLES = LOG2_SUBLANES + 1

EPS = 1e-6
BRANCH_WIDTH = 512
N_BRANCH = 3
S5_GROUP = 16
S5_STATE = 64
S5_EIG_MAX = -1e-4
S5_GROUPS_PER_TILE = LANES // S5_GROUP
S5_TILE_STATES = S5_GROUPS_PER_TILE * S5_STATE
S5_T = 8
S5_COL_BLOCK = MXU_WIDTH
HG_HEADS = 4
HG_DK = 128
HG_DV = 128
HG_CHUNK = 128
RG_BLOCKS = 8
RG_BLOCK = 64
RG_C = 8.0
CONV_WIDTH = 4

FFN_ROWS = 512
FFN_FCHUNK = 256
PROJ_ROWS = 512
SEQ_ROWS = 1024
HG_ROWS = 1024
MERGE_ROWS = 512


def _rmsnorm(x, w):
    ms = jnp.mean(x * x, axis=-1, keepdims=True)
    return x * lax.rsqrt(ms + EPS) * w


def _gelu(x):
    c0 = math.sqrt(2.0 / math.pi)
    hx = 0.5 * x
    return hx + hx * jnp.tanh(x * (c0 + (0.044715 * c0) * (x * x)))


def _const_spec(shape, single_buffer=False):
    nd = len(shape)
    kw = {"pipeline_mode": pl.Buffered(1)} if single_buffer else {}
    return pl.BlockSpec(shape, lambda *_: (0,) * nd, **kw)


def _stacked_spec(arr, lead, block_tail=None, tail_index=None):
    tail = tuple(arr.shape[len(lead):]) if block_tail is None else tuple(block_tail)
    idx = tuple(lead) + (tuple(tail_index) if tail_index is not None else (0,) * len(tail))
    return pl.BlockSpec((None,) * len(lead) + tail, lambda *_: idx, pipeline_mode=pl.Buffered(1))


def _params(*sem):
    return pltpu.CompilerParams(dimension_semantics=sem, vmem_limit_bytes=VMEM_LIMIT_BYTES)


def _ffn_body(x_ref, nw_ref, wg_ref, wu_ref, wd_ref, *rest, final):
    o_ref = rest[-1]
    x = x_ref[...]
    h = _rmsnorm(x, nw_ref[...]).astype(BF16)
    acc = jnp.zeros(x.shape, F32)
    for c in range(wg_ref.shape[1] // FFN_FCHUNK):
        cols = slice(c * FFN_FCHUNK, (c + 1) * FFN_FCHUNK)
        g = jnp.dot(h, wg_ref[:, cols].astype(BF16), preferred_element_type=F32)
        u = jnp.dot(h, wu_ref[:, cols].astype(BF16), preferred_element_type=F32)
        a = (g * jax.nn.sigmoid(g) * u).astype(BF16)
        acc = acc + jnp.dot(a, wd_ref[cols, :].astype(BF16), preferred_element_type=F32)
    y = x + 0.5 * acc
    if final:
        y = _rmsnorm(y, rest[0][...])
    o_ref[...] = y


def _ffn(x, nw, wg, wu, wd, which, final_w=None):
    n, d = x.shape
    final = final_w is not None
    in_specs = [
        pl.BlockSpec((FFN_ROWS, d), lambda i: (i, 0)),
        _const_spec((1, d)),
        _stacked_spec(wg, which),
        _stacked_spec(wu, which),
        _stacked_spec(wd, which),
    ]
    args = [x, nw, wg, wu, wd]
    if final:
        in_specs.append(_const_spec((1, d)))
        args.append(final_w)
    return pl.pallas_call(
        functools.partial(_ffn_body, final=final),
        grid=(n // FFN_ROWS,),
        in_specs=in_specs,
        out_specs=pl.BlockSpec((FFN_ROWS, d), lambda i: (i, 0)),
        out_shape=jax.ShapeDtypeStruct((n, d), F32),
        compiler_params=_params("parallel"),
        name="ffn_final" if final else "ffn",
    )(*args)


def _proj_body(x_ref, nw_ref, w_ref, o_ref):
    h = _rmsnorm(x_ref[...], nw_ref[...]).astype(BF16)
    o_ref[...] = jnp.dot(h, w_ref[...].astype(BF16), preferred_element_type=F32)


def _proj(x, nw, w_in, layer, cols):
    n, d = x.shape
    return pl.pallas_call(
        _proj_body,
        grid=(n // PROJ_ROWS,),
        in_specs=[
            pl.BlockSpec((PROJ_ROWS, d), lambda i: (i, 0)),
            _const_spec((1, d)),
            _stacked_spec(w_in, (layer,), block_tail=(d, cols), tail_index=(0, 0)),
        ],
        out_specs=pl.BlockSpec((PROJ_ROWS, cols), lambda i: (i, 0)),
        out_shape=jax.ShapeDtypeStruct((n, cols), F32),
        compiler_params=_params("parallel"),
        name="mixer_in_proj",
    )(x, nw, w_in)


def _bcast_last_row(v):
    return jnp.broadcast_to(v[SUBLANES - 1:SUBLANES, :], v.shape)


def _s5_build_tables(bmat_ref, ct_ref, pwr_ref, pwi_ref, mt_ref, g_ref, ht_ref):
    ts = S5_TILE_STATES
    bre, bim = bmat_ref[0, :, :ts], bmat_ref[0, :, ts:]
    cre, cim = ct_ref[0, :, :ts], ct_ref[0, :, ts:]
    cmb = jnp.concatenate([cre, -cim], axis=1).astype(BF16)
    zero_block = jnp.zeros((LANES, LANES), BF16)
    for tau in range(S5_T):
        pr, pi = pwr_ref[tau:tau + 1, :], pwi_ref[tau:tau + 1, :]
        bw = jnp.concatenate([bre * pr - bim * pi, bre * pi + bim * pr], axis=1)
        t_in = S5_T - 1 - tau
        g_ref[t_in * LANES:(t_in + 1) * LANES, :] = bw.astype(BF16)
        k_tau = lax.dot_general(bw.astype(BF16), cmb, (((1,), (1,)), ((), ())),
                                preferred_element_type=F32).astype(BF16)
        for t_in in range(S5_T - tau):
            t_out = t_in + tau
            mt_ref[t_in * LANES:(t_in + 1) * LANES, t_out * LANES:(t_out + 1) * LANES] = k_tau
        pr1, pi1 = pwr_ref[tau + 1:tau + 2, :], pwi_ref[tau + 1:tau + 2, :]
        ht_ref[tau * LANES:(tau + 1) * LANES, :] = jnp.concatenate(
            [cre * pr1 - cim * pi1, -(cre * pi1 + cim * pr1)], axis=1).astype(BF16)
    steps_per_block = S5_COL_BLOCK // LANES
    for t_out in range(S5_T):
        for t_in in range(t_out + 1, (t_out // steps_per_block + 1) * steps_per_block):
            mt_ref[t_in * LANES:(t_in + 1) * LANES, t_out * LANES:(t_out + 1) * LANES] = zero_block


def _s5_body(u_ref, bmat_ref, ct_ref, pwr_ref, pwi_ref, tre_ref, tim_ref, d_ref, o_ref,
             mt_ref, g_ref, ht_ref):
    @pl.when(pl.program_id(1) == 0)
    def _():
        _s5_build_tables(bmat_ref, ct_ref, pwr_ref, pwi_ref, mt_ref, g_ref, ht_ref)

    seq = u_ref.shape[0]
    nch = seq // S5_T
    ts = S5_TILE_STATES
    pieces = [u_ref[pl.ds(t, nch, stride=S5_T), :] for t in range(S5_T)]
    ucat = jnp.concatenate(pieces, axis=1).astype(BF16)
    x_end = jnp.dot(ucat, g_ref[...], preferred_element_type=F32)

    row_id = lax.broadcasted_iota(jnp.int32, (SUBLANES, ts), 0)
    cr = jnp.zeros((SUBLANES, ts), F32)
    ci = jnp.zeros((SUBLANES, ts), F32)
    pr = tre_ref[SCAN_CARRY]
    pi = tim_ref[SCAN_CARRY]
    prev_r, prev_i = [], []
    for k in range(nch // SUBLANES):
        rs = slice(k * SUBLANES, (k + 1) * SUBLANES)
        xr = x_end[rs, :ts]
        xi = x_end[rs, ts:]
        for s in range(LOG2_SUBLANES):
            ar = tre_ref[s]
            ai = tim_ref[s]
            sr = pltpu.roll(xr, 1 << s, axis=0)
            si = pltpu.roll(xi, 1 << s, axis=0)
            xr, xi = xr + ar * sr - ai * si, xi + ar * si + ai * sr
        xr, xi = xr + pr * cr - pi * ci, xi + pr * ci + pi * cr
        prev_r.append(jnp.where(row_id == 0, cr, pltpu.roll(xr, 1, axis=0)))
        prev_i.append(jnp.where(row_id == 0, ci, pltpu.roll(xi, 1, axis=0)))
        cr = _bcast_last_row(xr)
        ci = _bcast_last_row(xi)
    xprev = jnp.concatenate(
        [jnp.concatenate(prev_r, axis=0), jnp.concatenate(prev_i, axis=0)], axis=1).astype(BF16)
    y_carry = lax.dot_general(xprev, ht_ref[...], (((1,), (1,)), ((), ())),
                              preferred_element_type=F32)

    steps_per_block = S5_COL_BLOCK // LANES
    for c in range(S5_T // steps_per_block):
        cols = slice(c * S5_COL_BLOCK, (c + 1) * S5_COL_BLOCK)
        k_hi = (c + 1) * S5_COL_BLOCK
        y_c = y_carry[:, cols] + jnp.dot(ucat[:, :k_hi], mt_ref[:k_hi, cols],
                                         preferred_element_type=F32)
        for tt in range(steps_per_block):
            t = c * steps_per_block + tt
            y_t = y_c[:, tt * LANES:(tt + 1) * LANES] + d_ref[...] * pieces[t]
            o_ref[pl.ds(t, nch, stride=S5_T), :] = _gelu(y_t)


def _s5(proj, col_block, batch, seq, tabs, d_skip):
    n = proj.shape[0]
    n_tiles = BRANCH_WIDTH // LANES
    tl = S5_T * LANES
    ts = S5_TILE_STATES
    return pl.pallas_call(
        _s5_body,
        grid=(n_tiles, batch),
        in_specs=[
            pl.BlockSpec((seq, LANES), lambda j, b: (b, col_block * n_tiles + j)),
            pl.BlockSpec((1, LANES, 2 * ts), lambda j, b: (j, 0, 0)),
            pl.BlockSpec((1, LANES, 2 * ts), lambda j, b: (j, 0, 0)),
            pl.BlockSpec((S5_T + 1, ts), lambda j, b: (0, j)),
            pl.BlockSpec((S5_T + 1, ts), lambda j, b: (0, j)),
            pl.BlockSpec((SCAN_TABLES, SUBLANES, ts), lambda j, b: (0, 0, j)),
            pl.BlockSpec((SCAN_TABLES, SUBLANES, ts), lambda j, b: (0, 0, j)),
            pl.BlockSpec((1, LANES), lambda j, b: (0, j)),
        ],
        out_specs=pl.BlockSpec((seq, LANES), lambda j, b: (b, j)),
        out_shape=jax.ShapeDtypeStruct((n, BRANCH_WIDTH), F32),
        scratch_shapes=[
            pltpu.VMEM((tl, tl), BF16),
            pltpu.VMEM((tl, 2 * ts), BF16),
            pltpu.VMEM((tl, 2 * ts), BF16),
        ],
        compiler_params=_params("parallel", "arbitrary"),
        name="s5_mixer",
    )(proj, tabs["bmat"], tabs["ct"], tabs["pwr"], tabs["pwi"], tabs["tre"], tabs["tim"], d_skip)


def _rglru_body(x_ref, g_ref, cw_ref, cb_ref, wa_ref, ba_ref, wx_ref, bx_ref, c1_ref, o_ref,
                xbuf_ref, carry_ref):
    o_ref[...] = _rglru_rows(x_ref, g_ref, cw_ref, cb_ref, wa_ref, ba_ref, wx_ref, bx_ref, c1_ref,
                             xbuf_ref, carry_ref, pl.program_id(1) == 0)


def _rglru_rows(x_ref, g_ref, cw_ref, cb_ref, wa_ref, ba_ref, wx_ref, bx_ref, c1_ref,
                xbuf_ref, carry_ref, first):
    rows = x_ref.shape[0]

    @pl.when(first)
    def _():
        xbuf_ref[0:SUBLANES, :] = jnp.zeros((SUBLANES, x_ref.shape[1]), F32)
        carry_ref[...] = jnp.zeros(carry_ref.shape, F32)

    @pl.when(jnp.logical_not(first))
    def _():
        xbuf_ref[0:SUBLANES, :] = xbuf_ref[rows:rows + SUBLANES, :]

    xbuf_ref[SUBLANES:SUBLANES + rows, :] = x_ref[...]
    xc = cb_ref[...]
    for k in range(CONV_WIDTH):
        off = SUBLANES - (CONV_WIDTH - 1) + k
        xc = xc + cw_ref[k:k + 1, :] * xbuf_ref[off:off + rows, :]
    xcb = xc.astype(BF16)
    r = jax.nn.sigmoid(jnp.dot(xcb, wa_ref[...], preferred_element_type=F32) + ba_ref[...])
    i = jax.nn.sigmoid(jnp.dot(xcb, wx_ref[...], preferred_element_type=F32) + bx_ref[...])
    log_a = c1_ref[...] * r
    a_all = jnp.exp(log_a)
    b_all = jnp.sqrt(1.0 - a_all * a_all) * (i * xc)
    row_id = lax.broadcasted_iota(jnp.int32, (SUBLANES, x_ref.shape[1]), 0)
    carry = carry_ref[...]
    h_blocks = []
    for k in range(rows // SUBLANES):
        rs = slice(k * SUBLANES, (k + 1) * SUBLANES)
        a = a_all[rs]
        b = b_all[rs]
        for s in range(LOG2_SUBLANES):
            d = 1 << s
            m = row_id >= d
            sa = jnp.where(m, pltpu.roll(a, d, axis=0), 1.0)
            sb = jnp.where(m, pltpu.roll(b, d, axis=0), 0.0)
            a, b = a * sa, a * sb + b
        h = a * carry + b
        carry = _bcast_last_row(h)
        h_blocks.append(h)
    carry_ref[...] = carry
    return jnp.concatenate(h_blocks, axis=0) * _gelu(g_ref[...])


def _rglru(proj, x_block, gate_block, batch, seq, conv_w, conv_b, wa, ba, wx, bx, c1):
    n = proj.shape[0]
    bw = BRANCH_WIDTH
    steps = seq // SEQ_ROWS
    row_spec = lambda cb: pl.BlockSpec((SEQ_ROWS, bw), lambda b, l: (b * steps + l, cb))
    return pl.pallas_call(
        _rglru_body,
        grid=(batch, steps),
        in_specs=[
            row_spec(x_block), row_spec(gate_block),
            _const_spec((CONV_WIDTH, bw)), _const_spec((1, bw)),
            _const_spec((bw, bw)), _const_spec((1, bw)),
            _const_spec((bw, bw)), _const_spec((1, bw)),
            _const_spec((1, bw)),
        ],
        out_specs=pl.BlockSpec((SEQ_ROWS, bw), lambda b, l: (b * steps + l, 0)),
        out_shape=jax.ShapeDtypeStruct((n, bw), F32),
        scratch_shapes=[
            pltpu.VMEM((SEQ_ROWS + SUBLANES, bw), F32),
            pltpu.VMEM((SUBLANES, bw), F32),
        ],
        compiler_params=_params("parallel", "arbitrary"),
        name="rglru_mixer",
    )(proj, proj, conv_w, conv_b, wa, ba, wx, bx, c1)


def _rows_bcast(v, row):
    return jnp.broadcast_to(v[row:row + 1, :], (SUBLANES, v.shape[1]))


def _boundary_rows(b, m):
    c, w = b.shape
    if 2 * m >= SUBLANES:
        pieces = []
        for base in range(0, c, 2 * m):
            r = base + m - 1
            pieces.append(jnp.broadcast_to(b[r:r + 1, :], (2 * m, w)))
        return jnp.concatenate(pieces, axis=0)
    sub = lax.broadcasted_iota(jnp.int32, (SUBLANES, w), 0)
    pieces = []
    for base in range(0, c, SUBLANES):
        blk = b[base:base + SUBLANES, :]
        if m == 2:
            out = jnp.where(sub < 4, _rows_bcast(blk, 1), _rows_bcast(blk, 5))
        else:
            out = jnp.where(sub < 2, _rows_bcast(blk, 0),
                            jnp.where(sub < 4, _rows_bcast(blk, 2),
                                      jnp.where(sub < 6, _rows_bcast(blk, 4), _rows_bcast(blk, 6))))
        pieces.append(out)
    return jnp.concatenate(pieces, axis=0)


def _chunk_cumsum(x):
    sub = lax.broadcasted_iota(jnp.int32, (SUBLANES, x.shape[1]), 0)
    blocks = []
    carry = None
    for r in range(0, x.shape[0], SUBLANES):
        v = x[r:r + SUBLANES, :]
        for s in range(LOG2_SUBLANES):
            d = 1 << s
            v = v + jnp.where(sub >= d, pltpu.roll(v, d, axis=0), 0.0)
        if carry is not None:
            v = v + carry
        carry = _bcast_last_row(v)
        blocks.append(v)
    return jnp.concatenate(blocks, axis=0)


def _hgrn2_body(q_ref, f_ref, v_ref, g_ref, lb_ref, nw_ref, o_ref, st_ref):
    @pl.when(pl.program_id(1) == 0)
    def _():
        st_ref[...] = jnp.zeros(st_ref.shape, F32)

    rows = q_ref.shape[0]
    c = HG_CHUNK
    t_id = lax.broadcasted_iota(jnp.int32, (c, c), 0)
    s_id = lax.broadcasted_iota(jnp.int32, (c, c), 1)
    row_id = lax.broadcasted_iota(jnp.int32, (c, HG_DK), 0)
    levels = [1 << e for e in range(int(math.log2(c)))]
    pair_mask = {m: ((t_id >> (e + 1)) == (s_id >> (e + 1))) & ((t_id & m) != 0) & ((s_id & m) == 0)
                 for e, m in enumerate(levels)}
    right_half = {m: (row_id & m) != 0 for m in levels}

    for hd in range(HG_HEADS):
        lanes = slice(hd * HG_DK, (hd + 1) * HG_DK)
        lb = lb_ref[:, lanes]
        st = st_ref[hd]
        for ck in range(rows // c):
            rs = slice(ck * c, (ck + 1) * c)
            qraw = q_ref[rs, lanes]
            z = f_ref[rs, lanes]
            v = v_ref[rs, lanes]
            q = qraw * jax.nn.sigmoid(qraw)
            sig = jax.nn.sigmoid(z)
            f = lb + (1.0 - lb) * sig
            logf = jnp.log2(f)
            k = (1.0 - lb) * (1.0 - sig)
            b = _chunk_cumsum(logf)
            vb = v.astype(BF16)
            scores = jnp.zeros((c, c), F32)
            for m in levels:
                rh = right_half[m]
                if m == 1:
                    w = jnp.where(rh, f, 1.0)
                else:
                    d = b - _boundary_rows(b, m)
                    w = jnp.exp2(jnp.minimum(d, -d))
                xm = (jnp.where(rh, q, k) * w).astype(BF16)
                part = lax.dot_general(xm, xm, (((1,), (1,)), ((), ())), preferred_element_type=F32)
                scores = jnp.where(pair_mask[m], part, scores)
            o = jnp.dot(scores.astype(BF16), vb, preferred_element_type=F32)
            o = o + jnp.sum(q * k, axis=-1, keepdims=True) * v
            qd = (q * jnp.exp2(b)).astype(BF16)
            o = o + lax.dot_general(qd, st.astype(BF16), (((1,), (1,)), ((), ())),
                                    preferred_element_type=F32)
            b_last = b[c - 1:c, :]
            kd = (k * jnp.exp2(b_last - b)).astype(BF16)
            st = st * jnp.exp2(b_last) + lax.dot_general(
                vb, kd, (((0,), (0,)), ((), ())), preferred_element_type=F32)
            o = o * lax.rsqrt(jnp.mean(o * o, axis=-1, keepdims=True) + EPS) * nw_ref[:, lanes]
            graw = g_ref[rs, lanes]
            o_ref[rs, lanes] = o * (graw * jax.nn.sigmoid(graw))
        st_ref[hd] = st


def _hgrn2(proj, q_block, f_block, v_block, g_block, batch, seq, lb, norm_w):
    n = proj.shape[0]
    bw = BRANCH_WIDTH
    steps = seq // HG_ROWS
    row_spec = lambda cb: pl.BlockSpec((HG_ROWS, bw), lambda b, l: (b * steps + l, cb))
    return pl.pallas_call(
        _hgrn2_body,
        grid=(batch, steps),
        in_specs=[row_spec(q_block), row_spec(f_block), row_spec(v_block), row_spec(g_block),
                  _const_spec((1, bw)), _const_spec((1, bw))],
        out_specs=pl.BlockSpec((HG_ROWS, bw), lambda b, l: (b * steps + l, 0)),
        out_shape=jax.ShapeDtypeStruct((n, bw), F32),
        scratch_shapes=[pltpu.VMEM((HG_HEADS, HG_DV, HG_DK), F32)],
        compiler_params=_params("parallel", "arbitrary"),
        name="hgrn2_mixer",
    )(proj, proj, proj, proj, lb, norm_w)


def _merge_body(x_ref, nw_ref, za_ref, yb_ref, xc_ref, gc_ref, cw_ref, cb_ref, wa_ref, ba_ref, wx_ref,
                bx_ref, c1_ref, *rest, steps_per_seq):
    wg_refs, (gw_ref, gb_ref, p_ref, wo_ref, o_ref, xbuf_ref, carry_ref) = rest[:-7], rest[-7:]
    per_gate = len(wg_refs) // N_BRANCH
    half = x_ref.shape[0] // 2
    yc = _rglru_rows(xc_ref, gc_ref, cw_ref, cb_ref, wa_ref, ba_ref, wx_ref, bx_ref, c1_ref,
                     xbuf_ref, carry_ref, pl.program_id(0) % steps_per_seq == 0)
    wgs = [w[...].astype(BF16) for w in wg_refs]
    for r in range(2):
        rs = slice(r * half, (r + 1) * half)
        x = x_ref[rs, :]
        h = _rmsnorm(x, nw_ref[...]).astype(BF16)
        za = za_ref[rs, :]
        glu = jnp.dot(za.astype(BF16), gw_ref[...], preferred_element_type=F32) + gb_ref[...]
        ya = za * jax.nn.sigmoid(glu)
        merged = None
        for n, y in enumerate((ya, yb_ref[rs, :], yc[rs, :])):
            up = jnp.dot(y.astype(BF16), p_ref[n], preferred_element_type=F32)
            gate = jnp.concatenate(
                [jnp.dot(h, w, preferred_element_type=F32)
                 for w in wgs[n * per_gate:(n + 1) * per_gate]], axis=1)
            term = jax.nn.sigmoid(gate) * up
            merged = term if merged is None else merged + term
        o_ref[rs, :] = x + jnp.dot(merged.astype(BF16), wo_ref[...], preferred_element_type=F32)


def _merge(x, nw, za, yb, proj, xc_block, gc_block, seq, rg, w_in, first_gate_block, glu_w, glu_b, p, wo,
           layer):
    n, d = x.shape
    bw = BRANCH_WIDTH
    n_gate_blocks = N_BRANCH * d // bw
    y_spec = pl.BlockSpec((MERGE_ROWS, bw), lambda i: (i, 0))
    proj_spec = lambda cb: pl.BlockSpec((MERGE_ROWS, bw), lambda i: (i, cb))
    gate_specs = [_stacked_spec(w_in, (layer,), block_tail=(d, bw), tail_index=(0, first_gate_block + k))
                  for k in range(n_gate_blocks)]
    rg_specs = [_const_spec(a.shape) for a in rg]
    return pl.pallas_call(
        functools.partial(_merge_body, steps_per_seq=seq // MERGE_ROWS),
        grid=(n // MERGE_ROWS,),
        in_specs=[pl.BlockSpec((MERGE_ROWS, d), lambda i: (i, 0)), _const_spec((1, d)),
                  y_spec, y_spec, proj_spec(xc_block), proj_spec(gc_block)] + rg_specs + gate_specs
        + [_stacked_spec(glu_w, (layer,)), _const_spec((1, bw)),
           _stacked_spec(p, (layer,)), _stacked_spec(wo, (layer,))],
        out_specs=pl.BlockSpec((MERGE_ROWS, d), lambda i: (i, 0)),
        out_shape=jax.ShapeDtypeStruct((n, d), F32),
        scratch_shapes=[pltpu.VMEM((MERGE_ROWS + SUBLANES, bw), F32), pltpu.VMEM((SUBLANES, bw), F32)],
        compiler_params=_params("arbitrary"),
        name="merge_out_proj",
    )(x, nw, za, yb, proj, proj, *rg, *([w_in] * n_gate_blocks), glu_w, glu_b, p, wo)


def _s5_tables(lam_re, lam_im, log_dt, b_re, b_im, c_re, c_im):
    groups, states = lam_re.shape
    n_tiles = groups // S5_GROUPS_PER_TILE
    lr = jnp.minimum(lam_re.astype(F32), S5_EIG_MAX)
    li = lam_im.astype(F32)
    dt = jnp.exp(log_dt.astype(F32))[:, None]
    mag = jnp.exp(lr * dt)
    ar = mag * jnp.cos(li * dt)
    ai = mag * jnp.sin(li * dt)
    den = lr * lr + li * li
    fr = ((ar - 1.0) * lr + ai * li) / den
    fi = (ai * lr - (ar - 1.0) * li) / den
    br, bi = b_re.astype(F32), b_im.astype(F32)
    bbr = fr[..., None] * br - fi[..., None] * bi
    bbi = fr[..., None] * bi + fi[..., None] * br
    gpt = S5_GROUPS_PER_TILE
    eye = jnp.eye(gpt, dtype=F32)

    def in_tiles(m):
        m = m.reshape(n_tiles, gpt, states, S5_GROUP)
        return jnp.einsum('jgpc,gh->jgchp', m, eye).reshape(n_tiles, LANES, S5_TILE_STATES)

    def out_tiles(m):
        m = m.reshape(n_tiles, gpt, S5_GROUP, states)
        return jnp.einsum('jgcp,gh->jgchp', m, eye).reshape(n_tiles, LANES, S5_TILE_STATES)

    bmat = jnp.concatenate([in_tiles(bbr), in_tiles(bbi)], axis=2)
    ct = jnp.concatenate([out_tiles(c_re.astype(F32)), out_tiles(c_im.astype(F32))], axis=2)

    def power(kk):
        m = jnp.exp(lr * dt * kk)
        return m * jnp.cos(li * dt * kk), m * jnp.sin(li * dt * kk)

    pwr, pwi = power(jnp.arange(S5_T + 1, dtype=F32)[:, None, None])
    pwr = pwr.reshape(S5_T + 1, groups * states)
    pwi = pwi.reshape(S5_T + 1, groups * states)

    rows = jnp.arange(SUBLANES, dtype=F32)
    tre, tim = [], []
    for s in range(LOG2_SUBLANES):
        d = float(1 << s)
        sr, si = power(jnp.full((SUBLANES, 1, 1), d * S5_T, F32))
        keep = (rows >= d)[:, None, None]
        tre.append(jnp.where(keep, sr, 0.0))
        tim.append(jnp.where(keep, si, 0.0))
    sr, si = power((rows[:, None, None] + 1.0) * S5_T)
    tre.append(sr)
    tim.append(si)
    tre = jnp.stack(tre).reshape(SCAN_TABLES, SUBLANES, groups * states)
    tim = jnp.stack(tim).reshape(SCAN_TABLES, SUBLANES, groups * states)
    return {"bmat": bmat, "ct": ct, "pwr": pwr, "pwi": pwi, "tre": tre, "tim": tim}


def _block_diag(w):
    nb, bi, bo = w.shape
    eye = jnp.eye(nb, dtype=w.dtype)
    return jnp.einsum('hij,hk->hikj', w, eye).reshape(nb * bi, nb * bo)


def kernel(x, norm_w, final_norm_w, ffn_gate, ffn_up, ffn_down, w_in, branch_proj, w_out, s5_lambda_re, s5_lambda_im, s5_log_dt, s5_b_re, s5_b_im, s5_c_re, s5_c_im, s5_d, s5_glu_w, s5_glu_b, hg_lb_logits, hg_norm_w, rg_conv_w, rg_conv_b, rg_wa, rg_ba, rg_wx, rg_bx, rg_lambda):
    batch, seq, d = x.shape
    depth = norm_w.shape[0]
    col = {name: k for k, name in enumerate(("u", "q", "f", "v", "g", "xc", "gc"))}
    mix_cols = len(col) * BRANCH_WIDTH

    p = jax.nn.softmax(hg_lb_logits.astype(F32), axis=0)
    lower_bounds = jnp.cumsum(p, axis=0) - p[0]

    ffn_w = (ffn_gate, ffn_up, ffn_down)
    w_in_b = w_in
    glu_w_b, branch_proj_b, w_out_b = (s5_glu_w.astype(BF16), branch_proj.astype(BF16),
                                       w_out.astype(BF16))

    h = x.reshape(batch * seq, d)
    row = lambda v: v.reshape(1, -1).astype(F32)
    for l in range(depth):
        h = _ffn(h, row(norm_w[l, 0]), *ffn_w, (l, 0))
        proj = _proj(h, row(norm_w[l, 1]), w_in_b, l, mix_cols)
        tabs = _s5_tables(s5_lambda_re[l], s5_lambda_im[l], s5_log_dt[l], s5_b_re[l], s5_b_im[l],
                          s5_c_re[l], s5_c_im[l])
        za = _s5(proj, col["u"], batch, seq, tabs, row(s5_d[l]))
        yb = _hgrn2(proj, col["q"], col["f"], col["v"], col["g"], batch, seq,
                    row(lower_bounds[l]), row(hg_norm_w[l]))
        c1 = -RG_C * jax.nn.softplus(-rg_lambda[l].astype(F32))
        rg = (rg_conv_w[l].astype(F32), row(rg_conv_b[l]), _block_diag(rg_wa[l]).astype(BF16),
              row(rg_ba[l]), _block_diag(rg_wx[l]).astype(BF16), row(rg_bx[l]), row(c1))
        h = _merge(h, row(norm_w[l, 1]), za, yb, proj, col["xc"], col["gc"], seq, rg, w_in_b, len(col),
                   glu_w_b, row(s5_glu_b[l]), branch_proj_b, w_out_b, l)
        last = l == depth - 1
        h = _ffn(h, row(norm_w[l, 2]), *ffn_w, (l, 1),
                 final_w=row(final_norm_w) if last else None)
    return h.reshape(batch, seq, d)
```
